```python
import jax
import jax.numpy as jnp
from jax import lax
import numpy as np

D_MODEL = 1024
BATCH = 16
SEQ = 4096
DEPTH = 2

MIX_WIDTH = 512
N_BRANCHES = 3
EPS = 1e-6
FOX_HEADS = 4
FOX_HEAD_DIM = 128
FOX_BLOCK = 128
FOX_FORGET_BIAS = 2.0
GLA_HEADS = 4
GLA_DK = 64
GLA_DV = 128
GLA_KDIM = GLA_HEADS * GLA_DK
GLA_RANK = 16
GLA_TAU = 16.0
GLA_CHUNK = 64
RET_HEADS = 4
RET_DK = 128
RET_DV = 128
RET_CHUNK = 64
ROPE_BASE = 10000.0
IN_SPLITS = (MIX_WIDTH, MIX_WIDTH, MIX_WIDTH, FOX_HEADS,
             GLA_KDIM, GLA_KDIM, MIX_WIDTH, GLA_RANK, MIX_WIDTH,
             MIX_WIDTH, MIX_WIDTH, MIX_WIDTH, MIX_WIDTH)
D_IN = sum(IN_SPLITS)
D_FF = 2816
N_EXPERTS = 8
TOP_K = 2
D_FF_EXPERT = 3584
N_DENSE = (DEPTH + 1) // 2
N_MOE = DEPTH // 2

kernel_name = 'fox_gla_retnet_gated_hybrid_moe'


def rmsnorm(x, g):
    xf = x.astype(jnp.float32)
    y = xf * lax.rsqrt(jnp.mean(xf * xf, axis=-1, keepdims=True) + EPS)
    return (y * g.astype(jnp.float32)).astype(x.dtype)


def head_rmsnorm(o):
    return o * lax.rsqrt(jnp.mean(o * o, axis=-1, keepdims=True) + EPS)


def head_groupnorm(o):
    mu = jnp.mean(o, axis=-1, keepdims=True)
    var = jnp.mean(jnp.square(o - mu), axis=-1, keepdims=True)
    return (o - mu) * lax.rsqrt(var + EPS)


def split_heads(t, n):
    b, s, _ = t.shape
    return t.reshape(b, s, n, -1).transpose(0, 2, 1, 3)


def merge_heads(t):
    b, n, s, d = t.shape
    return t.transpose(0, 2, 1, 3).reshape(b, s, n * d)


def to_chunks(t, c):
    b, h, s, d = t.shape
    return t.reshape(b, h, s // c, c, d)


def rotary(t, pos):
    half = t.shape[-1] // 2
    inv_freq = ROPE_BASE ** (-(jnp.arange(half, dtype=jnp.float32) / half))
    ang = pos[:, None] * inv_freq[None, :]
    cos, sin = jnp.cos(ang), jnp.sin(ang)
    t1, t2 = t[..., :half], t[..., half:]
    return jnp.concatenate([t1 * cos - t2 * sin, t1 * sin + t2 * cos], axis=-1)


def fox_attention(q, k, v, log_f):
    s_len = q.shape[2]
    scale = q.shape[-1] ** -0.5
    c = jnp.cumsum(log_f, axis=-1)
    outs = []
    for i in range(s_len // FOX_BLOCK):
        lo = i * FOX_BLOCK
        hi = lo + FOX_BLOCK
        s = jnp.einsum('bhqd,bhkd->bhqk', q[:, :, lo:hi], k[:, :, :hi]).astype(jnp.float32) * scale
        s = s + c[:, :, lo:hi, None] - c[:, :, None, :hi]
        causal = (lo + jnp.arange(FOX_BLOCK))[:, None] >= jnp.arange(hi)[None, :]
        p = jax.nn.softmax(jnp.where(causal, s, -jnp.inf), axis=-1)
        outs.append(jnp.einsum('bhqk,bhkd->bhqd', p.astype(v.dtype), v[:, :, :hi]))
    return jnp.concatenate(outs, axis=2)


def gla_chunked(q, k, v, log_a):
    b_, h_, s_len, dk = q.shape
    dv = v.shape[-1]
    c = GLA_CHUNK
    qc, kc, vc = to_chunks(q, c), to_chunks(k, c), to_chunks(v, c)
    bcum = jnp.cumsum(to_chunks(log_a, c), axis=3)
    q_dec = qc * jnp.exp(bcum)
    k_inv = kc * jnp.exp(-bcum)
    k_end = kc * jnp.exp(bcum[..., -1:, :] - bcum)
    a_end = jnp.exp(bcum[..., -1, :])
    mask = jnp.tril(jnp.ones((c, c), dtype=bool))
    att = jnp.where(mask, jnp.einsum('bhnqd,bhnkd->bhnqk', q_dec, k_inv), 0.0)
    o_intra = jnp.einsum('bhnqk,bhnkv->bhnqv', att, vc)

    def step(state, inp):
        q_i, k_i, v_i, a_i = inp
        o = jnp.einsum('bhqd,bhdv->bhqv', q_i, state)
        state = a_i[..., :, None] * state + jnp.einsum('bhkd,bhkv->bhdv', k_i, v_i)
        return state, o

    xs = (jnp.moveaxis(q_dec, 2, 0), jnp.moveaxis(k_end, 2, 0),
          jnp.moveaxis(vc, 2, 0), jnp.moveaxis(a_end, 2, 0))
    _, o_inter = lax.scan(step, jnp.zeros((b_, h_, dk, dv), jnp.float32), xs)
    o = o_intra + jnp.moveaxis(o_inter, 0, 2)
    return o.reshape(b_, h_, s_len, dv)


def retention_chunked(q, k, v, log_gamma):
    b_, h_, s_len, dk = q.shape
    dv = v.shape[-1]
    c = RET_CHUNK
    qc, kc, vc = to_chunks(q, c), to_chunks(k, c), to_chunks(v, c)
    idx = jnp.arange(c, dtype=jnp.float32)
    diff = idx[:, None] - idx[None, :]
    lg = log_gamma[:, None, None]
    dmat = jnp.where(diff >= 0, jnp.exp(jnp.maximum(diff, 0.0) * lg), 0.0)
    inner = jnp.einsum('bhnqd,bhnkd->bhnqk', qc, kc) * dmat[None, :, None]
    o_intra = jnp.einsum('bhnqk,bhnkv->bhnqv', inner, vc)
    q_dec = qc * jnp.exp((idx + 1.0)[None, :] * log_gamma[:, None])[None, :, None, :, None]
    k_dec = kc * jnp.exp((c - 1.0 - idx)[None, :] * log_gamma[:, None])[None, :, None, :, None]
    chunk_dec = jnp.exp(c * log_gamma)[None, :, None, None]

    def step(state, inp):
        q_i, k_i, v_i = inp
        o = jnp.einsum('bhqd,bhdv->bhqv', q_i, state)
        state = chunk_dec * state + jnp.einsum('bhkd,bhkv->bhdv', k_i, v_i)
        return state, o

    xs = (jnp.moveaxis(q_dec, 2, 0), jnp.moveaxis(k_dec, 2, 0), jnp.moveaxis(vc, 2, 0))
    _, o_inter = lax.scan(step, jnp.zeros((b_, h_, dk, dv), jnp.float32), xs)
    o = o_intra + jnp.moveaxis(o_inter, 0, 2)
    return o.reshape(b_, h_, s_len, dv)


def hybrid_mixer(h, w_in, b_forget, w_gla_a2, b_gla_a, w_gate, w_branch, w_o):
    f32 = jnp.float32
    s_len = h.shape[1]
    proj = h @ w_in
    cuts = np.cumsum(np.array(IN_SPLITS))[:-1].tolist()
    (fq, fk, fv, fz, gq, gk, gv, ga, gr, rq, rk, rv, rg) = jnp.split(proj, cuts, axis=-1)

    log_f = jax.nn.log_sigmoid((fz + b_forget).astype(f32)).transpose(0, 2, 1)
    o_fox = fox_attention(split_heads(fq, FOX_HEADS), split_heads(fk, FOX_HEADS),
                          split_heads(fv, FOX_HEADS), log_f)
    o_fox = merge_heads(o_fox).astype(h.dtype)

    log_a = jax.nn.log_sigmoid((ga @ w_gla_a2 + b_gla_a).astype(f32)) / GLA_TAU
    o_gla = gla_chunked(split_heads(gq.astype(f32), GLA_HEADS) * GLA_DK ** -0.5,
                        split_heads(gk.astype(f32), GLA_HEADS),
                        split_heads(gv.astype(f32), GLA_HEADS),
                        split_heads(log_a, GLA_HEADS))
    o_gla = (merge_heads(head_rmsnorm(o_gla)) * jax.nn.silu(gr.astype(f32))).astype(h.dtype)

    pos = jnp.arange(s_len, dtype=f32)
    log_gamma = jnp.log1p(-jnp.exp2(-5.0 - jnp.arange(RET_HEADS, dtype=f32)))
    r_q = rotary(split_heads(rq.astype(f32), RET_HEADS), pos)
    r_k = rotary(split_heads(rk.astype(f32), RET_HEADS), pos) * RET_DK ** -0.5
    o_ret = retention_chunked(r_q, r_k, split_heads(rv.astype(f32), RET_HEADS), log_gamma)
    o_ret = (merge_heads(head_groupnorm(o_ret)) * jax.nn.silu(rg.astype(f32))).astype(h.dtype)

    merged = None
    for i, o_b in enumerate((o_fox, o_gla, o_ret)):
        gate = jax.nn.sigmoid(h @ w_gate[:, i * D_MODEL:(i + 1) * D_MODEL])
        term = gate * (o_b @ w_branch[i])
        merged = term if merged is None else merged + term
    return merged @ w_o


def swiglu(h, w1, w3, w2):
    return (jax.nn.silu(h @ w1) * (h @ w3)) @ w2


def moe_swiglu(h, router_w, w1, w3, w2):
    logits = (h @ router_w).astype(jnp.float32)
    top_v, top_i = lax.top_k(logits, TOP_K)
    top_w = jax.nn.softmax(top_v, axis=-1)
    combine = jnp.sum(jax.nn.one_hot(top_i, N_EXPERTS, dtype=jnp.float32) * top_w[..., None], axis=-2)
    out = jnp.zeros_like(h)
    for e in range(N_EXPERTS):
        out = out + combine[..., e:e + 1].astype(h.dtype) * swiglu(h, w1[e], w3[e], w2[e])
    return out


def setup_inputs(seed: int = 0) -> dict:
    key = jax.random.key(seed)
    ks = jax.random.split(key, 20)
    f32 = jnp.float32

    def nrm(k, shape, fan_in):
        return jax.random.normal(k, shape, f32) * fan_in ** -0.5

    def gain(k, shape):
        return 1.0 + 0.02 * jax.random.normal(k, shape, f32)

    return {
        'x': jax.random.normal(ks[0], (BATCH, SEQ, D_MODEL), f32),
        'norm1_g': gain(ks[1], (DEPTH, D_MODEL)),
        'w_in': nrm(ks[2], (DEPTH, D_MODEL, D_IN), D_MODEL),
        'b_forget': FOX_FORGET_BIAS + 0.1 * jax.random.normal(ks[3], (DEPTH, FOX_HEADS), f32),
        'w_gla_a2': nrm(ks[4], (DEPTH, GLA_RANK, GLA_KDIM), GLA_RANK),
        'b_gla_a': 0.1 * jax.random.normal(ks[5], (DEPTH, GLA_KDIM), f32),
        'w_gate': nrm(ks[6], (DEPTH, D_MODEL, N_BRANCHES * D_MODEL), D_MODEL),
        'w_branch': nrm(ks[7], (DEPTH, N_BRANCHES, MIX_WIDTH, D_MODEL), MIX_WIDTH),
        'w_o': nrm(ks[8], (DEPTH, D_MODEL, D_MODEL), D_MODEL),
        'norm2_g': gain(ks[9], (DEPTH, D_MODEL)),
        'ffn_w1': nrm(ks[10], (N_DENSE, D_MODEL, D_FF), D_MODEL),
        'ffn_w3': nrm(ks[11], (N_DENSE, D_MODEL, D_FF), D_MODEL),
        'ffn_w2': nrm(ks[12], (N_DENSE, D_FF, D_MODEL), D_FF),
        'router_w': nrm(ks[13], (N_MOE, D_MODEL, N_EXPERTS), D_MODEL),
        'moe_w1': nrm(ks[14], (N_MOE, N_EXPERTS, D_MODEL, D_FF_EXPERT), D_MODEL),
        'moe_w3': nrm(ks[15], (N_MOE, N_EXPERTS, D_MODEL, D_FF_EXPERT), D_MODEL),
        'moe_w2': nrm(ks[16], (N_MOE, N_EXPERTS, D_FF_EXPERT, D_MODEL), D_FF_EXPERT),
        'final_g': gain(ks[17], (D_MODEL,)),
    }


def reference(x, norm1_g, w_in, b_forget, w_gla_a2, b_gla_a, w_gate, w_branch, w_o,
              norm2_g, ffn_w1, ffn_w3, ffn_w2, router_w, moe_w1, moe_w3, moe_w2, final_g):
    for layer in range(DEPTH):
        h = rmsnorm(x, norm1_g[layer])
        x = x + hybrid_mixer(h, w_in[layer], b_forget[layer], w_gla_a2[layer], b_gla_a[layer],
                             w_gate[layer], w_branch[layer], w_o[layer])
        h = rmsnorm(x, norm2_g[layer])
        j = layer // 2
        if layer % 2 == 0:
            x = x + swiglu(h, ffn_w1[j], ffn_w3[j], ffn_w2[j])
        else:
            x = x + moe_swiglu(h, router_w[j], moe_w1[j], moe_w3[j], moe_w2[j])
    return rmsnorm(x, final_g)
```

```python
import functools
import math

import jax
import jax.numpy as jnp
from jax import lax
from jax.experimental import pallas as pl
from jax.experimental.pallas import tpu as pltpu

F32 = jnp.float32
BF16 = jnp.bfloat16

D_MODEL = 1024
N_HEADS = 4
HEAD_DV = 128
MIX_WIDTH = 512
GLA_DK = 64
GLA_KDIM = 256
GLA_RANK = 16
GLA_TAU = 16.0
GLA_CHUNK = 64
ROPE_BASE = 10000.0
NORM_EPS = 1e-6
N_EXPERTS = 8
LANES = 128
MASK_VALUE = -1e30
VMEM_LIMIT = 56 * 1024 * 1024

PROJ_WIDTH = 8192
PROJ_TILE = 1024
N_PLAIN_TILES = 5


def _params(*sem):
    return pltpu.CompilerParams(dimension_semantics=sem, vmem_limit_bytes=VMEM_LIMIT)


def _dot(a, b):
    return jnp.dot(a, b, preferred_element_type=F32)


def _dot_nt(a, b):
    return lax.dot_general(a, b, (((1,), (1,)), ((), ())), preferred_element_type=F32)


def _dot_tn(a, b):
    return lax.dot_general(a, b, (((0,), (0,)), ((), ())), preferred_element_type=F32)


def _split3(x):
    p1 = x.astype(BF16)
    r1 = x - p1.astype(F32)
    p2 = r1.astype(BF16)
    p3 = (r1 - p2.astype(F32)).astype(BF16)
    return p1, p2, p3


def _log_sigmoid(z):
    return jnp.minimum(z, 0.0) - jnp.log1p(jnp.exp(-jnp.abs(z)))


def _sigmoid(z):
    return 1.0 / (1.0 + jnp.exp(-z))


def _rmsnorm_rows(x, g):
    ms = jnp.mean(x * x, axis=-1, keepdims=True)
    return x * lax.rsqrt(ms + NORM_EPS) * g


def _norm_small_kernel(x_ref, g_ref, ws_ref, h_ref, s_ref):
    h = _rmsnorm_rows(x_ref[...], g_ref[...]).astype(BF16)
    h_ref[...] = h
    s_ref[...] = _dot(h, ws_ref[...])


def norm_small(x2d, g, w_small, tm=512):
    t = x2d.shape[0]
    return pl.pallas_call(
        _norm_small_kernel,
        grid=(t // tm,),
        in_specs=[
            pl.BlockSpec((tm, D_MODEL), lambda i: (i, 0)),
            pl.BlockSpec((1, D_MODEL), lambda i: (0, 0)),
            pl.BlockSpec((D_MODEL, LANES), lambda i: (0, 0)),
        ],
        out_specs=[
            pl.BlockSpec((tm, D_MODEL), lambda i: (i, 0)),
            pl.BlockSpec((tm, LANES), lambda i: (i, 0)),
        ],
        out_shape=[
            jax.ShapeDtypeStruct((t, D_MODEL), BF16),
            jax.ShapeDtypeStruct((t, LANES), F32),
        ],
        compiler_params=_params("parallel"),
        name="norm_small",
    )(x2d, g, w_small)


def _proj_kernel(h_ref, w_ref, o_ref):
    acc = _dot(h_ref[...], w_ref[...])
    j = pl.program_id(1)

    @pl.when(j < N_PLAIN_TILES)
    def _():
        o_ref[...] = acc.astype(BF16)

    @pl.when(j >= N_PLAIN_TILES)
    def _():
        o_ref[...] = _sigmoid(acc).astype(BF16)


def fused_proj(h, w_big, tm=1024):
    t = h.shape[0]
    return pl.pallas_call(
        _proj_kernel,
        grid=(t // tm, PROJ_WIDTH // PROJ_TILE),
        in_specs=[
            pl.BlockSpec((tm, D_MODEL), lambda i, j: (i, 0)),
            pl.BlockSpec((D_MODEL, PROJ_TILE), lambda i, j: (0, j)),
        ],
        out_specs=pl.BlockSpec((tm, PROJ_TILE), lambda i, j: (i, j)),
        out_shape=jax.ShapeDtypeStruct((t, PROJ_WIDTH), BF16),
        compiler_params=_params("parallel", "arbitrary"),
        name="fused_proj",
    )(h, w_big)


def _fox_prep_kernel(z_ref, b_ref, cq_ref, ck_ref, carry_ref):
    @pl.when(pl.program_id(1) == 0)
    def _():
        carry_ref[...] = jnp.zeros_like(carry_ref)

    ts = z_ref.shape[0]
    lf = _log_sigmoid(z_ref[...] + b_ref[...])
    r = lax.broadcasted_iota(jnp.int32, (ts, ts), 0)
    c = lax.broadcasted_iota(jnp.int32, (ts, ts), 1)
    tri = jnp.where(r >= c, 1.0, 0.0).astype(BF16)
    p1, p2, p3 = _split3(lf)
    cs = _dot(tri, p1) + _dot(tri, p2) + _dot(tri, p3) + carry_ref[...]
    carry_ref[...] = cs[ts - 1:ts, :]

    lane = lax.broadcasted_iota(jnp.int32, (ts, LANES), 1)
    cq_parts, ck_parts = [], []
    for h in range(N_HEADS):
        ch = jnp.broadcast_to(cs[:, h:h + 1], (ts, LANES))
        c1, c2, c3 = (p.astype(F32) for p in _split3(ch))
        cq = jnp.where(lane == 0, c1, jnp.where(lane == 1, c2, jnp.where(lane == 2, c3,
             jnp.where(lane < 6, 1.0, 0.0))))
        ck = jnp.where(lane < 3, 1.0, jnp.where(lane == 3, -c1, jnp.where(lane == 4, -c2,
             jnp.where(lane == 5, -c3, 0.0))))
        cq_parts.append(cq.astype(BF16))
        ck_parts.append(ck.astype(BF16))
    cq_ref[...] = jnp.concatenate(cq_parts, axis=1)
    ck_ref[...] = jnp.concatenate(ck_parts, axis=1)


def fox_prep(small3, bias_row, ts=512):
    b, s, _ = small3.shape
    ts = min(ts, s)
    return pl.pallas_call(
        _fox_prep_kernel,
        grid=(b, s // ts),
        in_specs=[
            pl.BlockSpec((None, ts, LANES), lambda bi, i: (bi, i, 0)),
            pl.BlockSpec((1, LANES), lambda bi, i: (0, 0)),
        ],
        out_specs=[
            pl.BlockSpec((None, ts, MIX_WIDTH), lambda bi, i: (bi, i, 0)),
            pl.BlockSpec((None, ts, MIX_WIDTH), lambda bi, i: (bi, i, 0)),
        ],
        out_shape=[
            jax.ShapeDtypeStruct((b, s, MIX_WIDTH), BF16),
            jax.ShapeDtypeStruct((b, s, MIX_WIDTH), BF16),
        ],
        scratch_shapes=[pltpu.VMEM((1, LANES), F32)],
        compiler_params=_params("parallel", "arbitrary"),
        name="fox_prep",
    )(small3, bias_row)


def _fox_kernel(q_ref, cq_ref, k_ref, ck_ref, v_ref, o_ref, *, tq):
    i = pl.program_id(2)
    q = jnp.concatenate([q_ref[...], cq_ref[...]], axis=1)

    def step(j, carry, masked):
        m, l, acc = carry
        off = pl.multiple_of(j * tq, tq)
        k = jnp.concatenate([k_ref[pl.ds(off, tq), :], ck_ref[pl.ds(off, tq), :]], axis=1)
        s = _dot_nt(q, k)
        if masked:
            r = lax.broadcasted_iota(jnp.int32, (tq, tq), 0)
            c = lax.broadcasted_iota(jnp.int32, (tq, tq), 1)
            s = jnp.where(r >= c, s, MASK_VALUE)
        m_new = jnp.maximum(m, jnp.max(s, axis=-1, keepdims=True))
        p = jnp.exp(s - m_new)
        alpha = jnp.exp(m - m_new)
        l = alpha * l + jnp.sum(p, axis=-1, keepdims=True)
        acc = alpha * acc + _dot(p.astype(BF16), v_ref[pl.ds(off, tq), :])
        return m_new, l, acc

    init = (jnp.full((tq, 1), MASK_VALUE, F32), jnp.zeros((tq, 1), F32),
            jnp.zeros((tq, HEAD_DV), F32))
    carry = lax.fori_loop(0, i, lambda j, cr: step(j, cr, False), init)
    _, l, acc = step(i, carry, True)
    o_ref[...] = (acc / l).astype(BF16)


def fox_attention(proj3, cq, ck, tq=512):
    b, s, _ = proj3.shape
    tq = min(tq, s)
    kernel = functools.partial(_fox_kernel, tq=tq)
    return pl.pallas_call(
        kernel,
        grid=(b, N_HEADS, s // tq),
        in_specs=[
            pl.BlockSpec((None, tq, HEAD_DV), lambda bi, h, i: (bi, i, h)),
            pl.BlockSpec((None, tq, HEAD_DV), lambda bi, h, i: (bi, i, h)),
            pl.BlockSpec((None, s, HEAD_DV), lambda bi, h, i: (bi, 0, N_HEADS + h)),
            pl.BlockSpec((None, s, HEAD_DV), lambda bi, h, i: (bi, 0, h)),
            pl.BlockSpec((None, s, HEAD_DV), lambda bi, h, i: (bi, 0, 2 * N_HEADS + h)),
        ],
        out_specs=pl.BlockSpec((None, tq, HEAD_DV), lambda bi, h, i: (bi, i, h)),
        out_shape=jax.ShapeDtypeStruct((b, s, MIX_WIDTH), BF16),
        compiler_params=_params("parallel", "parallel", "arbitrary"),
        name="fox_attention",
    )(proj3, cq, proj3, ck, proj3)


def _gla_kernel(q_ref, k_ref, v_ref, r_ref, sm_ref, wah_ref, wal_ref, ba_ref, o_ref, st_ref, *, blk):
    @pl.when(pl.program_id(1) == 0)
    def _():
        st_ref[...] = jnp.zeros_like(st_ref)

    n_chunks = blk // GLA_CHUNK
    ga = sm_ref[...]
    ga_hi = ga.astype(BF16)
    ga_lo = (ga - ga_hi.astype(F32)).astype(BF16)
    z = (_dot(ga_hi, wah_ref[...]) + _dot(ga_lo, wah_ref[...]) + _dot(ga_hi, wal_ref[...])
         + ba_ref[...])
    la = _log_sigmoid(z) * (1.0 / GLA_TAU)

    r = lax.broadcasted_iota(jnp.int32, (blk, blk), 0)
    c = lax.broadcasted_iota(jnp.int32, (blk, blk), 1)
    shift = GLA_CHUNK.bit_length() - 1
    same = jnp.right_shift(r, shift) == jnp.right_shift(c, shift)
    causal = (lax.bitcast_convert_type(r - c, jnp.uint32)
              <= lax.bitcast_convert_type(jnp.bitwise_and(r, GLA_CHUNK - 1), jnp.uint32))
    tri = jnp.where(causal, 1.0, 0.0).astype(BF16)
    ones = jnp.where(same, 1.0, 0.0).astype(BF16)
    p1, p2, p3 = _split3(la)
    bcum = _dot(tri, p1) + _dot(tri, p2) + _dot(tri, p3)
    btot = _dot(ones, p1) + _dot(ones, p2) + _dot(ones, p3)

    q = q_ref[...].astype(F32)
    k = k_ref[...].astype(F32)
    qd = (q * jnp.exp(bcum)).astype(BF16)
    ki = (k * jnp.exp(-bcum)).astype(BF16)
    ke = (k * jnp.exp(btot - bcum)).astype(BF16)
    a_end = jnp.exp(btot)
    v = v_ref[...]

    lane_head = lax.broadcasted_iota(jnp.int32, (blk, GLA_KDIM), 1) // GLA_DK
    o_intra = []
    for h in range(N_HEADS):
        qm = jnp.where(lane_head == h, qd, jnp.zeros_like(qd))
        att = _dot_nt(qm, ki)
        att = jnp.where(causal, att, 0.0).astype(BF16)
        o_intra.append(_dot(att, v[:, h * HEAD_DV:(h + 1) * HEAD_DV]))
    o_intra = jnp.concatenate(o_intra, axis=1)

    sr = lax.broadcasted_iota(jnp.int32, (MIX_WIDTH, GLA_KDIM), 0) // HEAD_DV
    sc = lax.broadcasted_iota(jnp.int32, (MIX_WIDTH, GLA_KDIM), 1) // GLA_DK
    diag = sr == sc
    st = st_ref[...]
    o_inter = []
    for ci in range(n_chunks):
        lo = ci * GLA_CHUNK
        o_inter.append(_dot_nt(qd[lo:lo + GLA_CHUNK], st.astype(BF16)))
        u = _dot_tn(v[lo:lo + GLA_CHUNK], ke[lo:lo + GLA_CHUNK])
        st = st * a_end[lo:lo + 1, :] + jnp.where(diag, u, 0.0)
    st_ref[...] = st
    o = o_intra + jnp.concatenate(o_inter, axis=0)

    gate = r_ref[...].astype(F32)
    gate = gate * _sigmoid(gate)
    outs = []
    for h in range(N_HEADS):
        oh = o[:, h * HEAD_DV:(h + 1) * HEAD_DV]
        oh = oh * lax.rsqrt(jnp.mean(oh * oh, axis=-1, keepdims=True) + NORM_EPS)
        outs.append(oh)
    o_ref[...] = (jnp.concatenate(outs, axis=1) * gate).astype(BF16)


def gla_mixer(proj3, small3, wa_hi, wa_lo, ba_row, blk=512):
    b, s, _ = proj3.shape
    blk = min(blk, s)
    kernel = functools.partial(_gla_kernel, blk=blk)
    return pl.pallas_call(
        kernel,
        grid=(b, s // blk),
        in_specs=[
            pl.BlockSpec((None, blk, GLA_KDIM), lambda bi, i: (bi, i, 6)),
            pl.BlockSpec((None, blk, GLA_KDIM), lambda bi, i: (bi, i, 7)),
            pl.BlockSpec((None, blk, MIX_WIDTH), lambda bi, i: (bi, i, 4)),
            pl.BlockSpec((None, blk, MIX_WIDTH), lambda bi, i: (bi, i, 5)),
            pl.BlockSpec((None, blk, LANES), lambda bi, i: (bi, i, 0)),
            pl.BlockSpec((LANES, GLA_KDIM), lambda bi, i: (0, 0)),
            pl.BlockSpec((LANES, GLA_KDIM), lambda bi, i: (0, 0)),
            pl.BlockSpec((1, GLA_KDIM), lambda bi, i: (0, 0)),
        ],
        out_specs=pl.BlockSpec((None, blk, MIX_WIDTH), lambda bi, i: (bi, i, 0)),
        out_shape=jax.ShapeDtypeStruct((b, s, MIX_WIDTH), BF16),
        scratch_shapes=[pltpu.VMEM((MIX_WIDTH, GLA_KDIM), F32)],
        compiler_params=_params("parallel", "arbitrary"),
        name="gla_mixer",
    )(proj3, proj3, proj3, proj3, small3, wa_hi, wa_lo, ba_row)


def _ret_kernel(q_ref, k_ref, v_ref, g_ref, cos_ref, sin_ref, o_ref, st_ref, *, blk):
    @pl.when(pl.program_id(1) == 0)
    def _():
        st_ref[...] = jnp.zeros_like(st_ref)

    row = lax.broadcasted_iota(jnp.int32, (blk, HEAD_DV), 0).astype(F32)
    r = lax.broadcasted_iota(jnp.int32, (blk, blk), 0)
    c = lax.broadcasted_iota(jnp.int32, (blk, blk), 1)
    causal = r >= c
    cosf = cos_ref[...]
    sinf = sin_ref[...]
    outs = []
    for h in range(N_HEADS):
        lg = math.log1p(-(2.0 ** (-5.0 - h)))
        sl = slice(h * HEAD_DV, (h + 1) * HEAD_DV)
        q = q_ref[:, sl].astype(F32)
        k = k_ref[:, sl].astype(F32)
        q = q * cosf + pltpu.roll(q, HEAD_DV // 2, 1) * sinf
        k = k * cosf + pltpu.roll(k, HEAD_DV // 2, 1) * sinf
        qd = (q * jnp.exp((row + 1.0) * lg)).astype(BF16)
        ki = (k * jnp.exp((row + 1.0) * (-lg))).astype(BF16)
        ke = (k * jnp.exp((blk - 1.0 - row) * lg)).astype(BF16)
        v = v_ref[:, sl]
        att = jnp.where(causal, _dot_nt(qd, ki), 0.0).astype(BF16)
        st = st_ref[h]
        o = _dot(att, v) + _dot(qd, st.astype(BF16))
        st_ref[h] = st * math.exp(blk * lg) + _dot_tn(ke, v)
        mu = jnp.mean(o, axis=-1, keepdims=True)
        d = o - mu
        var = jnp.mean(d * d, axis=-1, keepdims=True)
        gate = g_ref[:, sl].astype(F32)
        outs.append(d * lax.rsqrt(var + NORM_EPS) * (gate * _sigmoid(gate)))
    o_ref[...] = jnp.concatenate(outs, axis=1).astype(BF16)


def ret_mixer(proj3, cos_tab, sin_tab, blk=512):
    b, s, _ = proj3.shape
    blk = min(blk, s)
    kernel = functools.partial(_ret_kernel, blk=blk)
    return pl.pallas_call(
        kernel,
        grid=(b, s // blk),
        in_specs=[
            pl.BlockSpec((None, blk, MIX_WIDTH), lambda bi, i: (bi, i, 6)),
            pl.BlockSpec((None, blk, MIX_WIDTH), lambda bi, i: (bi, i, 7)),
            pl.BlockSpec((None, blk, MIX_WIDTH), lambda bi, i: (bi, i, 8)),
            pl.BlockSpec((None, blk, MIX_WIDTH), lambda bi, i: (bi, i, 9)),
            pl.BlockSpec((blk, HEAD_DV), lambda bi, i: (i, 0)),
            pl.BlockSpec((blk, HEAD_DV), lambda bi, i: (i, 0)),
        ],
        out_specs=pl.BlockSpec((None, blk, MIX_WIDTH), lambda bi, i: (bi, i, 0)),
        out_shape=jax.ShapeDtypeStruct((b, s, MIX_WIDTH), BF16),
        scratch_shapes=[pltpu.VMEM((N_HEADS, HEAD_DV, HEAD_DV), F32)],
        compiler_params=_params("parallel", "arbitrary"),
        name="ret_mixer",
    )(proj3, proj3, proj3, proj3, cos_tab, sin_tab)


def _mix_out_kernel(of_ref, og_ref, or_ref, g0_ref, g1_ref, g2_ref, x_ref, wb_ref, wo_ref, n_ref,
                    xo_ref, h_ref):
    m = (g0_ref[...].astype(F32) * _dot(of_ref[...], wb_ref[0])
         + g1_ref[...].astype(F32) * _dot(og_ref[...], wb_ref[1])
         + g2_ref[...].astype(F32) * _dot(or_ref[...], wb_ref[2]))
    xn = x_ref[...] + _dot(m.astype(BF16), wo_ref[...])
    xo_ref[...] = xn
    h_ref[...] = _rmsnorm_rows(xn, n_ref[...]).astype(BF16)


def mix_out(o_fox, o_gla, o_ret, proj, x2d, wb, wo, n2g, tm=512):
    t = x2d.shape[0]
    gate_blk = PROJ_TILE
    row = lambda i: (i, 0)
    return pl.pallas_call(
        _mix_out_kernel,
        grid=(t // tm,),
        in_specs=[
            pl.BlockSpec((tm, MIX_WIDTH), row),
            pl.BlockSpec((tm, MIX_WIDTH), row),
            pl.BlockSpec((tm, MIX_WIDTH), row),
            pl.BlockSpec((tm, gate_blk), lambda i: (i, N_PLAIN_TILES)),
            pl.BlockSpec((tm, gate_blk), lambda i: (i, N_PLAIN_TILES + 1)),
            pl.BlockSpec((tm, gate_blk), lambda i: (i, N_PLAIN_TILES + 2)),
            pl.BlockSpec((tm, D_MODEL), row),
            pl.BlockSpec((3, MIX_WIDTH, D_MODEL), lambda i: (0, 0, 0)),
            pl.BlockSpec((D_MODEL, D_MODEL), lambda i: (0, 0)),
            pl.BlockSpec((1, D_MODEL), lambda i: (0, 0)),
        ],
        out_specs=[pl.BlockSpec((tm, D_MODEL), row), pl.BlockSpec((tm, D_MODEL), row)],
        out_shape=[jax.ShapeDtypeStruct((t, D_MODEL), F32), jax.ShapeDtypeStruct((t, D_MODEL), BF16)],
        compiler_params=_params("parallel"),
        name="mix_out",
    )(o_fox, o_gla, o_ret, proj, proj, proj, x2d, wb, wo, n2g)


def _ffn_kernel(h_ref, w1_ref, w3_ref, w2_ref, x_ref, o_ref, acc_ref):
    j = pl.program_id(1)

    @pl.when(j == 0)
    def _():
        acc_ref[...] = jnp.zeros_like(acc_ref)

    h = h_ref[...]
    a = _dot(h, w1_ref[...])
    g = (a * _sigmoid(a) * _dot(h, w3_ref[...])).astype(BF16)
    acc_ref[...] += _dot(g, w2_ref[...])

    @pl.when(j == pl.num_programs(1) - 1)
    def _():
        o_ref[...] = x_ref[...] + acc_ref[...]


def dense_ffn(h, w1, w3, w2, x2d, tm=1024, tf=256):
    t = h.shape[0]
    d_ff = w1.shape[1]
    return pl.pallas_call(
        _ffn_kernel,
        grid=(t // tm, d_ff // tf),
        in_specs=[
            pl.BlockSpec((tm, D_MODEL), lambda i, j: (i, 0)),
            pl.BlockSpec((D_MODEL, tf), lambda i, j: (0, j)),
            pl.BlockSpec((D_MODEL, tf), lambda i, j: (0, j)),
            pl.BlockSpec((tf, D_MODEL), lambda i, j: (j, 0)),
            pl.BlockSpec((tm, D_MODEL), lambda i, j: (i, 0)),
        ],
        out_specs=pl.BlockSpec((tm, D_MODEL), lambda i, j: (i, 0)),
        out_shape=jax.ShapeDtypeStruct((t, D_MODEL), F32),
        scratch_shapes=[pltpu.VMEM((tm, D_MODEL), F32)],
        compiler_params=_params("parallel", "arbitrary"),
        name="dense_ffn",
    )(h, w1, w3, w2, x2d)


def _router_kernel(x_ref, g_ref, wh_ref, wl_ref, comb_ref):
    h = _rmsnorm_rows(x_ref[...], g_ref[...])
    h_hi = h.astype(BF16)
    h_lo = (h - h_hi.astype(F32)).astype(BF16)
    logits = _dot(h_hi, wh_ref[...]) + _dot(h_lo, wh_ref[...]) + _dot(h_hi, wl_ref[...])
    tm = logits.shape[0]
    lane = lax.broadcasted_iota(jnp.int32, (tm, LANES), 1)
    lg = jnp.where(lane < N_EXPERTS, logits, -jnp.inf)
    v1 = jnp.max(lg, axis=-1, keepdims=True)
    i1 = jnp.min(jnp.where(lg == v1, lane, LANES), axis=-1, keepdims=True)
    lg2 = jnp.where(lane == i1, -jnp.inf, lg)
    v2 = jnp.max(lg2, axis=-1, keepdims=True)
    i2 = jnp.min(jnp.where(lg2 == v2, lane, LANES), axis=-1, keepdims=True)
    e = jnp.exp(v2 - v1)
    w1 = 1.0 / (1.0 + e)
    w2 = e * w1
    comb_ref[...] = jnp.where(lane == i1, w1, jnp.where(lane == i2, w2, 0.0))


def moe_router(x2d, g, wr_hi, wr_lo, tm=512):
    t = x2d.shape[0]
    return pl.pallas_call(
        _router_kernel,
        grid=(t // tm,),
        in_specs=[
            pl.BlockSpec((tm, D_MODEL), lambda i: (i, 0)),
            pl.BlockSpec((1, D_MODEL), lambda i: (0, 0)),
            pl.BlockSpec((D_MODEL, LANES), lambda i: (0, 0)),
            pl.BlockSpec((D_MODEL, LANES), lambda i: (0, 0)),
        ],
        out_specs=pl.BlockSpec((tm, LANES), lambda i: (i, 0)),
        out_shape=jax.ShapeDtypeStruct((t, LANES), F32),
        compiler_params=_params("parallel"),
        name="moe_router",
    )(x2d, g, wr_hi, wr_lo)


def _moe_dense_kernel(h_ref, comb_ref, w1_ref, w3_ref, w2_ref, x_ref, fg_ref, o_ref, acc_ref, *, nf,
                      final):
    j = pl.program_id(1)

    @pl.when(j == 0)
    def _():
        acc_ref[...] = jnp.zeros_like(acc_ref)

    e = j // nf
    h = h_ref[...]
    a = _dot(h, w1_ref[...])
    g = (a * _sigmoid(a) * _dot(h, w3_ref[...])).astype(BF16)
    comb = comb_ref[...]
    lane = lax.broadcasted_iota(jnp.int32, comb.shape, 1)
    ce = jnp.sum(jnp.where(lane == e, comb, 0.0), axis=-1, keepdims=True)
    acc_ref[...] += ce * _dot(g, w2_ref[...])

    @pl.when(j == pl.num_programs(1) - 1)
    def _():
        y = x_ref[...] + acc_ref[...]
        o_ref[...] = _rmsnorm_rows(y, fg_ref[...]) if final else y


def moe_dense(h, comb, w1, w3, w2, x2d, final_g, final, tm=1024, tf=512):
    t = h.shape[0]
    d_ff = w1.shape[2]
    nf = d_ff // tf
    kernel = functools.partial(_moe_dense_kernel, nf=nf, final=final)
    return pl.pallas_call(
        kernel,
        grid=(t // tm, N_EXPERTS * nf),
        in_specs=[
            pl.BlockSpec((tm, D_MODEL), lambda i, j: (i, 0)),
            pl.BlockSpec((tm, LANES), lambda i, j: (i, 0)),
            pl.BlockSpec((None, D_MODEL, tf), lambda i, j: (j // nf, 0, j % nf)),
            pl.BlockSpec((None, D_MODEL, tf), lambda i, j: (j // nf, 0, j % nf)),
            pl.BlockSpec((None, tf, D_MODEL), lambda i, j: (j // nf, j % nf, 0)),
            pl.BlockSpec((tm, D_MODEL), lambda i, j: (i, 0)),
            pl.BlockSpec((1, D_MODEL), lambda i, j: (0, 0)),
        ],
        out_specs=pl.BlockSpec((tm, D_MODEL), lambda i, j: (i, 0)),
        out_shape=jax.ShapeDtypeStruct((t, D_MODEL), F32),
        scratch_shapes=[pltpu.VMEM((tm, D_MODEL), F32)],
        compiler_params=_params("parallel", "arbitrary"),
        name="moe_dense",
    )(h, comb, w1, w3, w2, x2d, final_g)


def _final_norm_kernel(x_ref, g_ref, o_ref):
    o_ref[...] = _rmsnorm_rows(x_ref[...], g_ref[...])


def final_norm(x2d, g, tm=512):
    t = x2d.shape[0]
    return pl.pallas_call(
        _final_norm_kernel,
        grid=(t // tm,),
        in_specs=[pl.BlockSpec((tm, D_MODEL), lambda i: (i, 0)),
                  pl.BlockSpec((1, D_MODEL), lambda i: (0, 0))],
        out_specs=pl.BlockSpec((tm, D_MODEL), lambda i: (i, 0)),
        out_shape=jax.ShapeDtypeStruct((t, D_MODEL), F32),
        compiler_params=_params("parallel"),
        name="final_norm",
    )(x2d, g)


_IN_SPLITS = (512, 512, 512, 4, 256, 256, 512, 16, 512, 512, 512, 512, 512)


def _prep_mixer_weights(w_in, w_gate, b_forget, w_gla_a2, b_gla_a):
    cuts = [0]
    for n in _IN_SPLITS:
        cuts.append(cuts[-1] + n)
    seg = [w_in[:, cuts[i]:cuts[i + 1]] for i in range(len(_IN_SPLITS))]
    fq, fk, fv, fz, gq, gk, gv, ga, gr, rq, rk, rv, rg = seg
    w_big = jnp.concatenate(
        [fq * (HEAD_DV ** -0.5), fk, fv,
         gq * (GLA_DK ** -0.5), gk, gv, gr,
         rq, rk * (HEAD_DV ** -0.5), rv, rg,
         w_gate], axis=1).astype(BF16)
    n_small = fz.shape[1] + ga.shape[1]
    w_small = jnp.concatenate(
        [fz, ga, jnp.zeros((D_MODEL, LANES - n_small), F32)], axis=1).astype(BF16)
    bias_row = jnp.zeros((1, LANES), F32).at[0, :N_HEADS].set(b_forget)
    wa = jnp.zeros((LANES, GLA_KDIM), F32).at[N_HEADS:N_HEADS + GLA_RANK, :].set(w_gla_a2)
    wa_hi = wa.astype(BF16)
    wa_lo = (wa - wa_hi.astype(F32)).astype(BF16)
    return w_big, w_small, bias_row, wa_hi, wa_lo, b_gla_a.reshape(1, GLA_KDIM)


def _rotary_tables(s_len):
    half = HEAD_DV // 2
    inv_freq = ROPE_BASE ** (-(jnp.arange(half, dtype=F32) / half))
    ang = jnp.arange(s_len, dtype=F32)[:, None] * inv_freq[None, :]
    cos, sin = jnp.cos(ang), jnp.sin(ang)
    return jnp.concatenate([cos, cos], axis=1), jnp.concatenate([-sin, sin], axis=1)


def kernel(x, norm1_g, w_in, b_forget, w_gla_a2, b_gla_a, w_gate, w_branch, w_o, norm2_g,
           ffn_w1, ffn_w3, ffn_w2, router_w, moe_w1, moe_w3, moe_w2, final_g):
    b, s, d = x.shape
    t = b * s
    depth = norm1_g.shape[0]
    cos_tab, sin_tab = _rotary_tables(s)
    x2d = x.reshape(t, d)
    out = None
    for layer in range(depth):
        w_big, w_small, bias_row, wa_hi, wa_lo, ba_row = _prep_mixer_weights(
            w_in[layer], w_gate[layer], b_forget[layer], w_gla_a2[layer], b_gla_a[layer])
        h, small = norm_small(x2d, norm1_g[layer].reshape(1, d), w_small)
        proj = fused_proj(h, w_big)
        proj3 = proj.reshape(b, s, PROJ_WIDTH)
        small3 = small.reshape(b, s, LANES)
        cq, ck = fox_prep(small3, bias_row)
        o_fox = fox_attention(proj3, cq, ck).reshape(t, MIX_WIDTH)
        o_gla = gla_mixer(proj3, small3, wa_hi, wa_lo, ba_row).reshape(t, MIX_WIDTH)
        o_ret = ret_mixer(proj3, cos_tab, sin_tab).reshape(t, MIX_WIDTH)
        x2d, h2 = mix_out(o_fox, o_gla, o_ret, proj, x2d, w_branch[layer].astype(BF16),
                          w_o[layer].astype(BF16), norm2_g[layer].reshape(1, d))
        j = layer // 2
        last = layer == depth - 1
        if layer % 2 == 0:
            x2d = dense_ffn(h2, ffn_w1[j].astype(BF16), ffn_w3[j].astype(BF16),
                            ffn_w2[j].astype(BF16), x2d)
            if last:
                out = final_norm(x2d, final_g.reshape(1, d))
        else:
            rw = jnp.zeros((d, LANES), F32).at[:, :N_EXPERTS].set(router_w[j])
            rw_hi = rw.astype(BF16)
            rw_lo = (rw - rw_hi.astype(F32)).astype(BF16)
            comb = moe_router(x2d, norm2_g[layer].reshape(1, d), rw_hi, rw_lo)
            x2d = moe_dense(h2, comb, moe_w1[j].astype(BF16), moe_w3[j].astype(BF16),
                            moe_w2[j].astype(BF16), x2d, final_g.reshape(1, d), last)
            out = x2d
    return out.reshape(b, s, d)
```

```python
import functools
import math

import jax
import jax.numpy as jnp
from jax import lax
from jax.experimental import pallas as pl
from jax.experimental.pallas import tpu as pltpu

F32 = jnp.float32
BF16 = jnp.bfloat16

D_MODEL = 1024
N_HEADS = 4
HEAD_DV = 128
MIX_WIDTH = 512
GLA_DK = 64
GLA_KDIM = 256
GLA_RANK = 16
GLA_TAU = 16.0
GLA_CHUNK = 64
ROPE_BASE = 10000.0
NORM_EPS = 1e-6
N_EXPERTS = 8
LANES = 128
MASK_VALUE = -1e30
VMEM_LIMIT = 56 * 1024 * 1024

PROJ_WIDTH = 8192
PROJ_TILE = 1024
N_PLAIN_TILES = 5


def _params(*sem):
    return pltpu.CompilerParams(dimension_semantics=sem, vmem_limit_bytes=VMEM_LIMIT)


def _dot(a, b):
    return jnp.dot(a, b, preferred_element_type=F32)


def _dot_nt(a, b):
    return lax.dot_general(a, b, (((1,), (1,)), ((), ())), preferred_element_type=F32)


def _dot_tn(a, b):
    return lax.dot_general(a, b, (((0,), (0,)), ((), ())), preferred_element_type=F32)


def _split3(x):
    p1 = x.astype(BF16)
    r1 = x - p1.astype(F32)
    p2 = r1.astype(BF16)
    p3 = (r1 - p2.astype(F32)).astype(BF16)
    return p1, p2, p3


def _log_sigmoid(z):
    return jnp.minimum(z, 0.0) - jnp.log1p(jnp.exp(-jnp.abs(z)))


def _sigmoid(z):
    return 1.0 / (1.0 + jnp.exp(-z))


def _rmsnorm_rows(x, g):
    ms = jnp.mean(x * x, axis=-1, keepdims=True)
    return x * lax.rsqrt(ms + NORM_EPS) * g


def _norm_small_kernel(x_ref, g_ref, ws_ref, h_ref, s_ref):
    h = _rmsnorm_rows(x_ref[...], g_ref[...]).astype(BF16)
    h_ref[...] = h
    s_ref[...] = _dot(h, ws_ref[...])


def norm_small(x2d, g, w_small, tm=512):
    t = x2d.shape[0]
    return pl.pallas_call(
        _norm_small_kernel,
        grid=(t // tm,),
        in_specs=[
            pl.BlockSpec((tm, D_MODEL), lambda i: (i, 0)),
            pl.BlockSpec((1, D_MODEL), lambda i: (0, 0)),
            pl.BlockSpec((D_MODEL, LANES), lambda i: (0, 0)),
        ],
        out_specs=[
            pl.BlockSpec((tm, D_MODEL), lambda i: (i, 0)),
            pl.BlockSpec((tm, LANES), lambda i: (i, 0)),
        ],
        out_shape=[
            jax.ShapeDtypeStruct((t, D_MODEL), BF16),
            jax.ShapeDtypeStruct((t, LANES), F32),
        ],
        compiler_params=_params("parallel"),
        name="norm_small",
    )(x2d, g, w_small)


def _proj_kernel(h_ref, w_ref, o_ref):
    acc = _dot(h_ref[...], w_ref[...])
    j = pl.program_id(1)

    @pl.when(j < N_PLAIN_TILES)
    def _():
        o_ref[...] = acc.astype(BF16)

    @pl.when(j >= N_PLAIN_TILES)
    def _():
        o_ref[...] = _sigmoid(acc).astype(BF16)


def fused_proj(h, w_big, tm=1024):
    t = h.shape[0]
    return pl.pallas_call(
        _proj_kernel,
        grid=(t // tm, PROJ_WIDTH // PROJ_TILE),
        in_specs=[
            pl.BlockSpec((tm, D_MODEL), lambda i, j: (i, 0)),
            pl.BlockSpec((D_MODEL, PROJ_TILE), lambda i, j: (0, j)),
        ],
        out_specs=pl.BlockSpec((tm, PROJ_TILE), lambda i, j: (i, j)),
        out_shape=jax.ShapeDtypeStruct((t, PROJ_WIDTH), BF16),
        compiler_params=_params("parallel", "arbitrary"),
        name="fused_proj",
    )(h, w_big)


def _fox_prep_kernel(z_ref, b_ref, cq_ref, ck_ref, carry_ref):
    @pl.when(pl.program_id(1) == 0)
    def _():
        carry_ref[...] = jnp.zeros_like(carry_ref)

    ts = z_ref.shape[0]
    lf = _log_sigmoid(z_ref[...] + b_ref[...])
    r = lax.broadcasted_iota(jnp.int32, (ts, ts), 0)
    c = lax.broadcasted_iota(jnp.int32, (ts, ts), 1)
    tri = jnp.where(r >= c, 1.0, 0.0).astype(BF16)
    p1, p2, p3 = _split3(lf)
    cs = _dot(tri, p1) + _dot(tri, p2) + _dot(tri, p3) + carry_ref[...]
    carry_ref[...] = cs[ts - 1:ts, :]

    lane = lax.broadcasted_iota(jnp.int32, (ts, LANES), 1)
    cq_parts, ck_parts = [], []
    for h in range(N_HEADS):
        ch = jnp.broadcast_to(cs[:, h:h + 1], (ts, LANES))
        c1, c2, c3 = (p.astype(F32) for p in _split3(ch))
        cq = jnp.where(lane == 0, c1, jnp.where(lane == 1, c2, jnp.where(lane == 2, c3,
             jnp.where(lane < 6, 1.0, 0.0))))
        ck = jnp.where(lane < 3, 1.0, jnp.where(lane == 3, -c1, jnp.where(lane == 4, -c2,
             jnp.where(lane == 5, -c3, 0.0))))
        cq_parts.append(cq.astype(BF16))
        ck_parts.append(ck.astype(BF16))
    cq_ref[...] = jnp.concatenate(cq_parts, axis=1)
    ck_ref[...] = jnp.concatenate(ck_parts, axis=1)


def fox_prep(small3, bias_row, ts=512):
    b, s, _ = small3.shape
    ts = min(ts, s)
    return pl.pallas_call(
        _fox_prep_kernel,
        grid=(b, s // ts),
        in_specs=[
            pl.BlockSpec((None, ts, LANES), lambda bi, i: (bi, i, 0)),
            pl.BlockSpec((1, LANES), lambda bi, i: (0, 0)),
        ],
        out_specs=[
            pl.BlockSpec((None, ts, MIX_WIDTH), lambda bi, i: (bi, i, 0)),
            pl.BlockSpec((None, ts, MIX_WIDTH), lambda bi, i: (bi, i, 0)),
        ],
        out_shape=[
            jax.ShapeDtypeStruct((b, s, MIX_WIDTH), BF16),
            jax.ShapeDtypeStruct((b, s, MIX_WIDTH), BF16),
        ],
        scratch_shapes=[pltpu.VMEM((1, LANES), F32)],
        compiler_params=_params("parallel", "arbitrary"),
        name="fox_prep",
    )(small3, bias_row)


def _fox_kernel(q_ref, cq_ref, k_ref, ck_ref, v_ref, o_ref, *, tq):
    i = pl.program_id(2)
    q = jnp.concatenate([q_ref[...], cq_ref[...]], axis=1)

    def step(j, carry, masked):
        m, l, acc = carry
        off = pl.multiple_of(j * tq, tq)
        k = jnp.concatenate([k_ref[pl.ds(off, tq), :], ck_ref[pl.ds(off, tq), :]], axis=1)
        s = _dot_nt(q, k)
        if masked:
            r = lax.broadcasted_iota(jnp.int32, (tq, tq), 0)
            c = lax.broadcasted_iota(jnp.int32, (tq, tq), 1)
            s = jnp.where(r >= c, s, MASK_VALUE)
        m_new = jnp.maximum(m, jnp.max(s, axis=-1, keepdims=True))
        p = jnp.exp(s - m_new)
        alpha = jnp.exp(m - m_new)
        l = alpha * l + jnp.sum(p, axis=-1, keepdims=True)
        acc = alpha * acc + _dot(p.astype(BF16), v_ref[pl.ds(off, tq), :])
        return m_new, l, acc

    init = (jnp.full((tq, 1), MASK_VALUE, F32), jnp.zeros((tq, 1), F32),
            jnp.zeros((tq, HEAD_DV), F32))
    carry = lax.fori_loop(0, i, lambda j, cr: step(j, cr, False), init)
    _, l, acc = step(i, carry, True)
    o_ref[...] = (acc / l).astype(BF16)


def fox_attention(proj3, cq, ck, tq=512):
    b, s, _ = proj3.shape
    tq = min(tq, s)
    kernel = functools.partial(_fox_kernel, tq=tq)
    return pl.pallas_call(
        kernel,
        grid=(b, N_HEADS, s // tq),
        in_specs=[
            pl.BlockSpec((None, tq, HEAD_DV), lambda bi, h, i: (bi, i, h)),
            pl.BlockSpec((None, tq, HEAD_DV), lambda bi, h, i: (bi, i, h)),
            pl.BlockSpec((None, s, HEAD_DV), lambda bi, h, i: (bi, 0, N_HEADS + h)),
            pl.BlockSpec((None, s, HEAD_DV), lambda bi, h, i: (bi, 0, h)),
            pl.BlockSpec((None, s, HEAD_DV), lambda bi, h, i: (bi, 0, 2 * N_HEADS + h)),
        ],
        out_specs=pl.BlockSpec((None, tq, HEAD_DV), lambda bi, h, i: (bi, i, h)),
        out_shape=jax.ShapeDtypeStruct((b, s, MIX_WIDTH), BF16),
        compiler_params=_params("parallel", "parallel", "arbitrary"),
        name="fox_attention",
    )(proj3, cq, proj3, ck, proj3)


def _gla_kernel(q_ref, k_ref, v_ref, r_ref, sm_ref, wah_ref, wal_ref, ba_ref, o_ref, st_ref, *, blk):
    @pl.when(pl.program_id(1) == 0)
    def _():
        st_ref[...] = jnp.zeros_like(st_ref)

    n_chunks = blk // GLA_CHUNK
    ga = sm_ref[...]
    ga_hi = ga.astype(BF16)
    ga_lo = (ga - ga_hi.astype(F32)).astype(BF16)
    z = (_dot(ga_hi, wah_ref[...]) + _dot(ga_lo, wah_ref[...]) + _dot(ga_hi, wal_ref[...])
         + ba_ref[...])
    la = _log_sigmoid(z) * (1.0 / GLA_TAU)

    r = lax.broadcasted_iota(jnp.int32, (blk, blk), 0)
    c = lax.broadcasted_iota(jnp.int32, (blk, blk), 1)
    shift = GLA_CHUNK.bit_length() - 1
    same = jnp.right_shift(r, shift) == jnp.right_shift(c, shift)
    causal = (lax.bitcast_convert_type(r - c, jnp.uint32)
              <= lax.bitcast_convert_type(jnp.bitwise_and(r, GLA_CHUNK - 1), jnp.uint32))
    tri = jnp.where(causal, 1.0, 0.0).astype(BF16)
    ones = jnp.where(same, 1.0, 0.0).astype(BF16)
    p1, p2, p3 = _split3(la)
    bcum = _dot(tri, p1) + _dot(tri, p2) + _dot(tri, p3)
    btot = _dot(ones, p1) + _dot(ones, p2) + _dot(ones, p3)

    q = q_ref[...].astype(F32)
    k = k_ref[...].astype(F32)
    qd = (q * jnp.exp(bcum)).astype(BF16)
    ki = (k * jnp.exp(-bcum)).astype(BF16)
    ke = (k * jnp.exp(btot - bcum)).astype(BF16)
    a_end = jnp.exp(btot)
    v = v_ref[...]

    lane_head = lax.broadcasted_iota(jnp.int32, (blk, GLA_KDIM), 1) // GLA_DK
    o_intra = []
    for h in range(N_HEADS):
        qm = jnp.where(lane_head == h, qd, jnp.zeros_like(qd))
        att = _dot_nt(qm, ki)
        att = jnp.where(causal, att, 0.0).astype(BF16)
        o_intra.append(_dot(att, v[:, h * HEAD_DV:(h + 1) * HEAD_DV]))
    o_intra = jnp.concatenate(o_intra, axis=1)

    sr = lax.broadcasted_iota(jnp.int32, (MIX_WIDTH, GLA_KDIM), 0) // HEAD_DV
    sc = lax.broadcasted_iota(jnp.int32, (MIX_WIDTH, GLA_KDIM), 1) // GLA_DK
    diag = sr == sc
    st = st_ref[...]
    o_inter = []
    for ci in range(n_chunks):
        lo = ci * GLA_CHUNK
        o_inter.append(_dot_nt(qd[lo:lo + GLA_CHUNK], st.astype(BF16)))
        u = _dot_tn(v[lo:lo + GLA_CHUNK], ke[lo:lo + GLA_CHUNK])
        st = st * a_end[lo:lo + 1, :] + jnp.where(diag, u, 0.0)
    st_ref[...] = st
    o = o_intra + jnp.concatenate(o_inter, axis=0)

    gate = r_ref[...].astype(F32)
    gate = gate * _sigmoid(gate)
    outs = []
    for h in range(N_HEADS):
        oh = o[:, h * HEAD_DV:(h + 1) * HEAD_DV]
        oh = oh * lax.rsqrt(jnp.mean(oh * oh, axis=-1, keepdims=True) + NORM_EPS)
        outs.append(oh)
    o_ref[...] = (jnp.concatenate(outs, axis=1) * gate).astype(BF16)


def gla_mixer(proj3, small3, wa_hi, wa_lo, ba_row, blk=512):
    b, s, _ = proj3.shape
    blk = min(blk, s)
    kernel = functools.partial(_gla_kernel, blk=blk)
    return pl.pallas_call(
        kernel,
        grid=(b, s // blk),
        in_specs=[
            pl.BlockSpec((None, blk, GLA_KDIM), lambda bi, i: (bi, i, 6)),
            pl.BlockSpec((None, blk, GLA_KDIM), lambda bi, i: (bi, i, 7)),
            pl.BlockSpec((None, blk, MIX_WIDTH), lambda bi, i: (bi, i, 4)),
            pl.BlockSpec((None, blk, MIX_WIDTH), lambda bi, i: (bi, i, 5)),
            pl.BlockSpec((None, blk, LANES), lambda bi, i: (bi, i, 0)),
            pl.BlockSpec((LANES, GLA_KDIM), lambda bi, i: (0, 0)),
            pl.BlockSpec((LANES, GLA_KDIM), lambda bi, i: (0, 0)),
            pl.BlockSpec((1, GLA_KDIM), lambda bi, i: (0, 0)),
        ],
        out_specs=pl.BlockSpec((None, blk, MIX_WIDTH), lambda bi, i: (bi, i, 0)),
        out_shape=jax.ShapeDtypeStruct((b, s, MIX_WIDTH), BF16),
        scratch_shapes=[pltpu.VMEM((MIX_WIDTH, GLA_KDIM), F32)],
        compiler_params=_params("parallel", "arbitrary"),
        name="gla_mixer",
    )(proj3, proj3, proj3, proj3, small3, wa_hi, wa_lo, ba_row)


def _ret_kernel(q_ref, k_ref, v_ref, g_ref, cos_ref, sin_ref, o_ref, st_ref, *, blk):
    @pl.when(pl.program_id(1) == 0)
    def _():
        st_ref[...] = jnp.zeros_like(st_ref)

    row = lax.broadcasted_iota(jnp.int32, (blk, HEAD_DV), 0).astype(F32)
    r = lax.broadcasted_iota(jnp.int32, (blk, blk), 0)
    c = lax.broadcasted_iota(jnp.int32, (blk, blk), 1)
    causal = r >= c
    cosf = cos_ref[...]
    sinf = sin_ref[...]
    outs = []
    for h in range(N_HEADS):
        lg = math.log1p(-(2.0 ** (-5.0 - h)))
        sl = slice(h * HEAD_DV, (h + 1) * HEAD_DV)
        q = q_ref[:, sl].astype(F32)
        k = k_ref[:, sl].astype(F32)
        q = q * cosf + pltpu.roll(q, HEAD_DV // 2, 1) * sinf
        k = k * cosf + pltpu.roll(k, HEAD_DV // 2, 1) * sinf
        qd = (q * jnp.exp((row + 1.0) * lg)).astype(BF16)
        ki = (k * jnp.exp((row + 1.0) * (-lg))).astype(BF16)
        ke = (k * jnp.exp((blk - 1.0 - row) * lg)).astype(BF16)
        v = v_ref[:, sl]
        att = jnp.where(causal, _dot_nt(qd, ki), 0.0).astype(BF16)
        st = st_ref[h]
        o = _dot(att, v) + _dot(qd, st.astype(BF16))
        st_ref[h] = st * math.exp(blk * lg) + _dot_tn(ke, v)
        mu = jnp.mean(o, axis=-1, keepdims=True)
        d = o - mu
        var = jnp.mean(d * d, axis=-1, keepdims=True)
        gate = g_ref[:, sl].astype(F32)
        outs.append(d * lax.rsqrt(var + NORM_EPS) * (gate * _sigmoid(gate)))
    o_ref[...] = jnp.concatenate(outs, axis=1).astype(BF16)


def ret_mixer(proj3, cos_tab, sin_tab, blk=512):
    b, s, _ = proj3.shape
    blk = min(blk, s)
    kernel = functools.partial(_ret_kernel, blk=blk)
    return pl.pallas_call(
        kernel,
        grid=(b, s // blk),
        in_specs=[
            pl.BlockSpec((None, blk, MIX_WIDTH), lambda bi, i: (bi, i, 6)),
            pl.BlockSpec((None, blk, MIX_WIDTH), lambda bi, i: (bi, i, 7)),
            pl.BlockSpec((None, blk, MIX_WIDTH), lambda bi, i: (bi, i, 8)),
            pl.BlockSpec((None, blk, MIX_WIDTH), lambda bi, i: (bi, i, 9)),
            pl.BlockSpec((blk, HEAD_DV), lambda bi, i: (i, 0)),
            pl.BlockSpec((blk, HEAD_DV), lambda bi, i: (i, 0)),
        ],
        out_specs=pl.BlockSpec((None, blk, MIX_WIDTH), lambda bi, i: (bi, i, 0)),
        out_shape=jax.ShapeDtypeStruct((b, s, MIX_WIDTH), BF16),
        scratch_shapes=[pltpu.VMEM((N_HEADS, HEAD_DV, HEAD_DV), F32)],
        compiler_params=_params("parallel", "arbitrary"),
        name="ret_mixer",
    )(proj3, proj3, proj3, proj3, cos_tab, sin_tab)


def _mix_out_kernel(of_ref, og_ref, or_ref, g0_ref, g1_ref, g2_ref, x_ref, wb_ref, wo_ref, n_ref,
                    xo_ref, h_ref):
    m = (g0_ref[...].astype(F32) * _dot(of_ref[...], wb_ref[0])
         + g1_ref[...].astype(F32) * _dot(og_ref[...], wb_ref[1])
         + g2_ref[...].astype(F32) * _dot(or_ref[...], wb_ref[2]))
    xn = x_ref[...] + _dot(m.astype(BF16), wo_ref[...])
    xo_ref[...] = xn
    h_ref[...] = _rmsnorm_rows(xn, n_ref[...]).astype(BF16)


def mix_out(o_fox, o_gla, o_ret, proj, x2d, wb, wo, n2g, tm=512):
    t = x2d.shape[0]
    gate_blk = PROJ_TILE
    row = lambda i: (i, 0)
    return pl.pallas_call(
        _mix_out_kernel,
        grid=(t // tm,),
        in_specs=[
            pl.BlockSpec((tm, MIX_WIDTH), row),
            pl.BlockSpec((tm, MIX_WIDTH), row),
            pl.BlockSpec((tm, MIX_WIDTH), row),
            pl.BlockSpec((tm, gate_blk), lambda i: (i, N_PLAIN_TILES)),
            pl.BlockSpec((tm, gate_blk), lambda i: (i, N_PLAIN_TILES + 1)),
            pl.BlockSpec((tm, gate_blk), lambda i: (i, N_PLAIN_TILES + 2)),
            pl.BlockSpec((tm, D_MODEL), row),
            pl.BlockSpec((3, MIX_WIDTH, D_MODEL), lambda i: (0, 0, 0)),
            pl.BlockSpec((D_MODEL, D_MODEL), lambda i: (0, 0)),
            pl.BlockSpec((1, D_MODEL), lambda i: (0, 0)),
        ],
        out_specs=[pl.BlockSpec((tm, D_MODEL), row), pl.BlockSpec((tm, D_MODEL), row)],
        out_shape=[jax.ShapeDtypeStruct((t, D_MODEL), F32), jax.ShapeDtypeStruct((t, D_MODEL), BF16)],
        compiler_params=_params("parallel"),
        name="mix_out",
    )(o_fox, o_gla, o_ret, proj, proj, proj, x2d, wb, wo, n2g)


def _ffn_kernel(h_ref, w1_ref, w3_ref, w2_ref, x_ref, o_ref, acc_ref):
    j = pl.program_id(1)

    @pl.when(j == 0)
    def _():
        acc_ref[...] = jnp.zeros_like(acc_ref)

    h = h_ref[...]
    a = _dot(h, w1_ref[...])
    g = (a * _sigmoid(a) * _dot(h, w3_ref[...])).astype(BF16)
    acc_ref[...] += _dot(g, w2_ref[...])

    @pl.when(j == pl.num_programs(1) - 1)
    def _():
        o_ref[...] = x_ref[...] + acc_ref[...]


def dense_ffn(h, w1, w3, w2, x2d, tm=1024, tf=256):
    t = h.shape[0]
    d_ff = w1.shape[1]
    return pl.pallas_call(
        _ffn_kernel,
        grid=(t // tm, d_ff // tf),
        in_specs=[
            pl.BlockSpec((tm, D_MODEL), lambda i, j: (i, 0)),
            pl.BlockSpec((D_MODEL, tf), lambda i, j: (0, j)),
            pl.BlockSpec((D_MODEL, tf), lambda i, j: (0, j)),
            pl.BlockSpec((tf, D_MODEL), lambda i, j: (j, 0)),
            pl.BlockSpec((tm, D_MODEL), lambda i, j: (i, 0)),
        ],
        out_specs=pl.BlockSpec((tm, D_MODEL), lambda i, j: (i, 0)),
        out_shape=jax.ShapeDtypeStruct((t, D_MODEL), F32),
        scratch_shapes=[pltpu.VMEM((tm, D_MODEL), F32)],
        compiler_params=_params("parallel", "arbitrary"),
        name="dense_ffn",
    )(h, w1, w3, w2, x2d)


def _router_kernel(x_ref, g_ref, wh_ref, wl_ref, mi_ref, mf_ref, cnt_ref, run_ref):
    @pl.when(pl.program_id(0) == 0)
    def _():
        run_ref[...] = jnp.zeros_like(run_ref)

    h = _rmsnorm_rows(x_ref[...], g_ref[...])
    h_hi = h.astype(BF16)
    h_lo = (h - h_hi.astype(F32)).astype(BF16)
    logits = _dot(h_hi, wh_ref[...]) + _dot(h_lo, wh_ref[...]) + _dot(h_hi, wl_ref[...])
    tm = logits.shape[0]
    lane = lax.broadcasted_iota(jnp.int32, (tm, LANES), 1)
    lg = jnp.where(lane < N_EXPERTS, logits, -jnp.inf)
    v1 = jnp.max(lg, axis=-1, keepdims=True)
    i1 = jnp.min(jnp.where(lg == v1, lane, LANES), axis=-1, keepdims=True)
    lg2 = jnp.where(lane == i1, -jnp.inf, lg)
    v2 = jnp.max(lg2, axis=-1, keepdims=True)
    i2 = jnp.min(jnp.where(lg2 == v2, lane, LANES), axis=-1, keepdims=True)
    e = jnp.exp(v2 - v1)
    w1 = 1.0 / (1.0 + e)
    w2 = e * w1

    oh1 = lane == i1
    oh2 = lane == i2
    cnt = jnp.where(oh1, 1.0, 0.0) + jnp.where(oh2, 1.0, 0.0)
    r = lax.broadcasted_iota(jnp.int32, (tm, tm), 0)
    c = lax.broadcasted_iota(jnp.int32, (tm, tm), 1)
    earlier = jnp.where(r > c, 1.0, 0.0).astype(BF16)
    excl = _dot(earlier, cnt.astype(BF16)) + run_ref[...]
    r1 = jnp.sum(jnp.where(oh1, excl, 0.0), axis=-1, keepdims=True)
    r2 = jnp.sum(jnp.where(oh2, excl, 0.0), axis=-1, keepdims=True)
    run_ref[...] += jnp.sum(cnt, axis=0, keepdims=True)
    cnt_ref[...] = run_ref[...]

    mi_ref[...] = jnp.where(lane == 0, i1, jnp.where(lane == 1, i2, jnp.where(
        lane == 2, r1.astype(jnp.int32), jnp.where(lane == 3, r2.astype(jnp.int32), 0))))
    mf_ref[...] = jnp.where(lane == 0, w1, jnp.where(lane == 1, w2, 0.0))


def moe_router(x2d, g, wr_hi, wr_lo, tm=512):
    t = x2d.shape[0]
    return pl.pallas_call(
        _router_kernel,
        grid=(t // tm,),
        in_specs=[
            pl.BlockSpec((tm, D_MODEL), lambda i: (i, 0)),
            pl.BlockSpec((1, D_MODEL), lambda i: (0, 0)),
            pl.BlockSpec((D_MODEL, LANES), lambda i: (0, 0)),
            pl.BlockSpec((D_MODEL, LANES), lambda i: (0, 0)),
        ],
        out_specs=[
            pl.BlockSpec((tm, LANES), lambda i: (i, 0)),
            pl.BlockSpec((tm, LANES), lambda i: (i, 0)),
            pl.BlockSpec((1, LANES), lambda i: (0, 0)),
        ],
        out_shape=[
            jax.ShapeDtypeStruct((t, LANES), jnp.int32),
            jax.ShapeDtypeStruct((t, LANES), F32),
            jax.ShapeDtypeStruct((1, LANES), F32),
        ],
        scratch_shapes=[pltpu.VMEM((1, LANES), F32)],
        compiler_params=_params("arbitrary"),
        name="moe_router",
    )(x2d, g, wr_hi, wr_lo)


def _row_copy(src_ref, src_row, dst_ref, dst_row, sem):
    return pltpu.make_async_copy(src_ref.at[pl.ds(src_row, 1), :], dst_ref.at[pl.ds(dst_row, 1), :], sem)


def _dispatch_kernel(pos_ref, x_hbm, xs_init_hbm, xs_hbm, sem, *, td):
    del xs_init_hbm
    base = pl.program_id(0) * td

    def issue(t, carry):
        _row_copy(x_hbm, base + t, xs_hbm, pos_ref[0, t], sem).start()
        _row_copy(x_hbm, base + t, xs_hbm, pos_ref[0, td + t], sem).start()
        return carry

    lax.fori_loop(0, td, issue, 0)

    def drain(t, carry):
        _row_copy(x_hbm, 0, xs_hbm, 0, sem).wait()
        _row_copy(x_hbm, 0, xs_hbm, 0, sem).wait()
        return carry

    lax.fori_loop(0, td, drain, 0)


def moe_dispatch(pos_tiles, x2d, n_rows, td):
    t = x2d.shape[0]
    xs_init = jnp.zeros((n_rows, D_MODEL), F32)
    kernel = functools.partial(_dispatch_kernel, td=td)
    return pl.pallas_call(
        kernel,
        grid=(t // td,),
        in_specs=[
            pl.BlockSpec((None, 1, 2 * td), lambda i: (i, 0, 0), memory_space=pltpu.SMEM),
            pl.BlockSpec(memory_space=pl.ANY),
            pl.BlockSpec(memory_space=pl.ANY),
        ],
        out_specs=pl.BlockSpec(memory_space=pl.ANY),
        out_shape=jax.ShapeDtypeStruct((n_rows, D_MODEL), F32),
        scratch_shapes=[pltpu.SemaphoreType.DMA(())],
        input_output_aliases={2: 0},
        compiler_params=_params("arbitrary"),
        name="moe_dispatch",
    )(pos_tiles, x2d, xs_init)


def _moe_grouped_kernel(te_ref, nu_ref, xs_ref, g_ref, w1_ref, w3_ref, w2_ref, ys_ref, h_ref, acc_ref):
    del te_ref
    j = pl.program_id(1)
    used = pl.program_id(0) < nu_ref[0]

    @pl.when(jnp.logical_and(jnp.logical_not(used), j == pl.num_programs(1) - 1))
    def _():
        ys_ref[...] = jnp.zeros_like(ys_ref)

    @pl.when(used)
    def _():
        @pl.when(j == 0)
        def _():
            h_ref[...] = _rmsnorm_rows(xs_ref[...], g_ref[...]).astype(BF16)
            acc_ref[...] = jnp.zeros_like(acc_ref)

        h = h_ref[...]
        a = _dot(h, w1_ref[...])
        g = (a * _sigmoid(a) * _dot(h, w3_ref[...])).astype(BF16)
        acc_ref[...] += _dot(g, w2_ref[...])

        @pl.when(j == pl.num_programs(1) - 1)
        def _():
            ys_ref[...] = acc_ref[...]


def moe_grouped(tile_expert, n_used, xs, g, w1, w3, w2, tm, tf=512):
    n_rows = xs.shape[0]
    nf = w1.shape[2] // tf

    def row_map(i, j, te, nu):
        return (jnp.minimum(i, nu[0] - 1), 0)

    def chunk(i, j, nu):
        return jnp.where(i < nu[0], j, nf - 1)

    return pl.pallas_call(
        _moe_grouped_kernel,
        grid_spec=pltpu.PrefetchScalarGridSpec(
            num_scalar_prefetch=2,
            grid=(n_rows // tm, nf),
            in_specs=[
                pl.BlockSpec((tm, D_MODEL), row_map),
                pl.BlockSpec((1, D_MODEL), lambda i, j, te, nu: (0, 0)),
                pl.BlockSpec((None, D_MODEL, tf), lambda i, j, te, nu: (te[i], 0, chunk(i, j, nu))),
                pl.BlockSpec((None, D_MODEL, tf), lambda i, j, te, nu: (te[i], 0, chunk(i, j, nu))),
                pl.BlockSpec((None, tf, D_MODEL), lambda i, j, te, nu: (te[i], chunk(i, j, nu), 0)),
            ],
            out_specs=pl.BlockSpec((tm, D_MODEL), lambda i, j, te, nu: (i, 0)),
            scratch_shapes=[pltpu.VMEM((tm, D_MODEL), BF16), pltpu.VMEM((tm, D_MODEL), F32)],
        ),
        out_shape=jax.ShapeDtypeStruct((n_rows, D_MODEL), F32),
        compiler_params=_params("arbitrary", "arbitrary"),
        name="moe_grouped",
    )(tile_expert, n_used, xs, g, w1, w3, w2)


def _combine_kernel(pos_ref, ys_hbm, x_ref, mf_ref, fg_ref, o_ref, buf_ref, sem, *, tc, final):
    def issue(t, carry):
        _row_copy(ys_hbm, pos_ref[0, t], buf_ref.at[0], t, sem).start()
        _row_copy(ys_hbm, pos_ref[0, tc + t], buf_ref.at[1], t, sem).start()
        return carry

    lax.fori_loop(0, tc, issue, 0)

    def drain(t, carry):
        _row_copy(ys_hbm, 0, buf_ref.at[0], 0, sem).wait()
        _row_copy(ys_hbm, 0, buf_ref.at[1], 0, sem).wait()
        return carry

    lax.fori_loop(0, tc, drain, 0)

    mf = mf_ref[...]
    y = x_ref[...] + mf[:, 0:1] * buf_ref[0] + mf[:, 1:2] * buf_ref[1]
    o_ref[...] = _rmsnorm_rows(y, fg_ref[...]) if final else y


def moe_combine(pos_tiles, ys, x2d, mf, final_g, final, tc):
    t = x2d.shape[0]
    kernel = functools.partial(_combine_kernel, tc=tc, final=final)
    return pl.pallas_call(
        kernel,
        grid=(t // tc,),
        in_specs=[
            pl.BlockSpec((None, 1, 2 * tc), lambda i: (i, 0, 0), memory_space=pltpu.SMEM),
            pl.BlockSpec(memory_space=pl.ANY),
            pl.BlockSpec((tc, D_MODEL), lambda i: (i, 0)),
            pl.BlockSpec((tc, LANES), lambda i: (i, 0)),
            pl.BlockSpec((1, D_MODEL), lambda i: (0, 0)),
        ],
        out_specs=pl.BlockSpec((tc, D_MODEL), lambda i: (i, 0)),
        out_shape=jax.ShapeDtypeStruct((t, D_MODEL), F32),
        scratch_shapes=[pltpu.VMEM((2, tc, D_MODEL), F32), pltpu.SemaphoreType.DMA(())],
        compiler_params=_params("arbitrary"),
        name="moe_combine",
    )(pos_tiles, ys, x2d, mf, final_g)


MOE_ROW_TILE = 1024
MOE_COPY_TILE = 256


def routed_moe(x2d, h_gain, rw_hi, rw_lo, w1, w3, w2, final_g, final):
    t = x2d.shape[0]
    tm = min(MOE_ROW_TILE, t)
    tcp = min(MOE_COPY_TILE, t)
    mi, mf, counts = moe_router(x2d, h_gain, rw_hi, rw_lo)

    cnt = counts[0, :N_EXPERTS].astype(jnp.int32)
    tiles_e = (cnt + tm - 1) // tm
    cum = jnp.cumsum(tiles_e)
    row_off = (cum - tiles_e) * tm
    n_used = cum[-1:]
    n_tiles = (2 * t) // tm + N_EXPERTS
    tile_ids = jnp.minimum(jnp.arange(n_tiles, dtype=jnp.int32), n_used[0] - 1)
    tile_expert = jnp.searchsorted(cum, tile_ids, side="right").astype(jnp.int32)
    pos1 = row_off[mi[:, 0]] + mi[:, 2]
    pos2 = row_off[mi[:, 1]] + mi[:, 3]
    pos_tiles = jnp.concatenate(
        [pos1.reshape(t // tcp, 1, tcp), pos2.reshape(t // tcp, 1, tcp)], axis=2).astype(jnp.int32)

    xs = moe_dispatch(pos_tiles, x2d, n_tiles * tm, tcp)
    ys = moe_grouped(tile_expert, n_used.astype(jnp.int32), xs, h_gain, w1, w3, w2, tm)
    return moe_combine(pos_tiles, ys, x2d, mf, final_g, final, tcp)


def _final_norm_kernel(x_ref, g_ref, o_ref):
    o_ref[...] = _rmsnorm_rows(x_ref[...], g_ref[...])


def final_norm(x2d, g, tm=512):
    t = x2d.shape[0]
    return pl.pallas_call(
        _final_norm_kernel,
        grid=(t // tm,),
        in_specs=[pl.BlockSpec((tm, D_MODEL), lambda i: (i, 0)),
                  pl.BlockSpec((1, D_MODEL), lambda i: (0, 0))],
        out_specs=pl.BlockSpec((tm, D_MODEL), lambda i: (i, 0)),
        out_shape=jax.ShapeDtypeStruct((t, D_MODEL), F32),
        compiler_params=_params("parallel"),
        name="final_norm",
    )(x2d, g)


_IN_SPLITS = (512, 512, 512, 4, 256, 256, 512, 16, 512, 512, 512, 512, 512)


def _prep_mixer_weights(w_in, w_gate, b_forget, w_gla_a2, b_gla_a):
    cuts = [0]
    for n in _IN_SPLITS:
        cuts.append(cuts[-1] + n)
    seg = [w_in[:, cuts[i]:cuts[i + 1]] for i in range(len(_IN_SPLITS))]
    fq, fk, fv, fz, gq, gk, gv, ga, gr, rq, rk, rv, rg = seg
    w_big = jnp.concatenate(
        [fq * (HEAD_DV ** -0.5), fk, fv,
         gq * (GLA_DK ** -0.5), gk, gv, gr,
         rq, rk * (HEAD_DV ** -0.5), rv, rg,
         w_gate], axis=1).astype(BF16)
    n_small = fz.shape[1] + ga.shape[1]
    w_small = jnp.concatenate(
        [fz, ga, jnp.zeros((D_MODEL, LANES - n_small), F32)], axis=1).astype(BF16)
    bias_row = jnp.zeros((1, LANES), F32).at[0, :N_HEADS].set(b_forget)
    wa = jnp.zeros((LANES, GLA_KDIM), F32).at[N_HEADS:N_HEADS + GLA_RANK, :].set(w_gla_a2)
    wa_hi = wa.astype(BF16)
    wa_lo = (wa - wa_hi.astype(F32)).astype(BF16)
    return w_big, w_small, bias_row, wa_hi, wa_lo, b_gla_a.reshape(1, GLA_KDIM)


def _rotary_tables(s_len):
    half = HEAD_DV // 2
    inv_freq = ROPE_BASE ** (-(jnp.arange(half, dtype=F32) / half))
    ang = jnp.arange(s_len, dtype=F32)[:, None] * inv_freq[None, :]
    cos, sin = jnp.cos(ang), jnp.sin(ang)
    return jnp.concatenate([cos, cos], axis=1), jnp.concatenate([-sin, sin], axis=1)


def kernel(x, norm1_g, w_in, b_forget, w_gla_a2, b_gla_a, w_gate, w_branch, w_o, norm2_g,
           ffn_w1, ffn_w3, ffn_w2, router_w, moe_w1, moe_w3, moe_w2, final_g):
    b, s, d = x.shape
    t = b * s
    depth = norm1_g.shape[0]
    cos_tab, sin_tab = _rotary_tables(s)
    x2d = x.reshape(t, d)
    out = None
    for layer in range(depth):
        w_big, w_small, bias_row, wa_hi, wa_lo, ba_row = _prep_mixer_weights(
            w_in[layer], w_gate[layer], b_forget[layer], w_gla_a2[layer], b_gla_a[layer])
        h, small = norm_small(x2d, norm1_g[layer].reshape(1, d), w_small)
        proj = fused_proj(h, w_big)
        proj3 = proj.reshape(b, s, PROJ_WIDTH)
        small3 = small.reshape(b, s, LANES)
        cq, ck = fox_prep(small3, bias_row)
        o_fox = fox_attention(proj3, cq, ck).reshape(t, MIX_WIDTH)
        o_gla = gla_mixer(proj3, small3, wa_hi, wa_lo, ba_row).reshape(t, MIX_WIDTH)
        o_ret = ret_mixer(proj3, cos_tab, sin_tab).reshape(t, MIX_WIDTH)
        x2d, h2 = mix_out(o_fox, o_gla, o_ret, proj, x2d, w_branch[layer].astype(BF16),
                          w_o[layer].astype(BF16), norm2_g[layer].reshape(1, d))
        j = layer // 2
        last = layer == depth - 1
        if layer % 2 == 0:
            x2d = dense_ffn(h2, ffn_w1[j].astype(BF16), ffn_w3[j].astype(BF16),
                            ffn_w2[j].astype(BF16), x2d)
            if last:
                out = final_norm(x2d, final_g.reshape(1, d))
        else:
            rw = jnp.zeros((d, LANES), F32).at[:, :N_EXPERTS].set(router_w[j])
            rw_hi = rw.astype(BF16)
            rw_lo = (rw - rw_hi.astype(F32)).astype(BF16)
            x2d = routed_moe(x2d, norm2_g[layer].reshape(1, d), rw_hi, rw_lo,
                             moe_w1[j].astype(BF16), moe_w3[j].astype(BF16),
                             moe_w2[j].astype(BF16), final_g.reshape(1, d), last)
            out = x2d
    return out.reshape(b, s, d)
```

```python
import functools
import math

import jax
import jax.numpy as jnp
from jax import lax
from jax.experimental import pallas as pl
from jax.experimental.pallas import tpu as pltpu

F32 = jnp.float32
BF16 = jnp.bfloat16

D_MODEL = 1024
N_HEADS = 4
HEAD_DV = 128
MIX_WIDTH = 512
GLA_DK = 64
GLA_KDIM = 256
GLA_RANK = 16
GLA_TAU = 16.0
GLA_CHUNK = 64
ROPE_BASE = 10000.0
NORM_EPS = 1e-6
N_EXPERTS = 8
LANES = 128
MASK_VALUE = -1e30
VMEM_LIMIT = 56 * 1024 * 1024

PROJ_WIDTH = 8192
PROJ_TILE = 1024
N_PLAIN_TILES = 5


def _params(*sem):
    return pltpu.CompilerParams(dimension_semantics=sem, vmem_limit_bytes=VMEM_LIMIT)


def _dot(a, b):
    return jnp.dot(a, b, preferred_element_type=F32)


def _dot_nt(a, b):
    return lax.dot_general(a, b, (((1,), (1,)), ((), ())), preferred_element_type=F32)


def _dot_tn(a, b):
    return lax.dot_general(a, b, (((0,), (0,)), ((), ())), preferred_element_type=F32)


def _split3(x):
    p1 = x.astype(BF16)
    r1 = x - p1.astype(F32)
    p2 = r1.astype(BF16)
    p3 = (r1 - p2.astype(F32)).astype(BF16)
    return p1, p2, p3


def _log_sigmoid(z):
    return jnp.minimum(z, 0.0) - jnp.log1p(jnp.exp(-jnp.abs(z)))


def _sigmoid(z):
    return 1.0 / (1.0 + jnp.exp(-z))


def _rmsnorm_rows(x, g):
    ms = jnp.mean(x * x, axis=-1, keepdims=True)
    return x * lax.rsqrt(ms + NORM_EPS) * g


def _norm_small_kernel(x_ref, g_ref, ws_ref, h_ref, s_ref):
    h = _rmsnorm_rows(x_ref[...], g_ref[...]).astype(BF16)
    h_ref[...] = h
    s_ref[...] = _dot(h, ws_ref[...])


def norm_small(x2d, g, w_small, tm=512):
    t = x2d.shape[0]
    return pl.pallas_call(
        _norm_small_kernel,
        grid=(t // tm,),
        in_specs=[
            pl.BlockSpec((tm, D_MODEL), lambda i: (i, 0)),
            pl.BlockSpec((1, D_MODEL), lambda i: (0, 0)),
            pl.BlockSpec((D_MODEL, LANES), lambda i: (0, 0)),
        ],
        out_specs=[
            pl.BlockSpec((tm, D_MODEL), lambda i: (i, 0)),
            pl.BlockSpec((tm, LANES), lambda i: (i, 0)),
        ],
        out_shape=[
            jax.ShapeDtypeStruct((t, D_MODEL), BF16),
            jax.ShapeDtypeStruct((t, LANES), F32),
        ],
        compiler_params=_params("parallel"),
        name="norm_small",
    )(x2d, g, w_small)


def _proj_kernel(h_ref, w_ref, o_ref):
    acc = _dot(h_ref[...], w_ref[...])
    j = pl.program_id(1)

    @pl.when(j < N_PLAIN_TILES)
    def _():
        o_ref[...] = acc.astype(BF16)

    @pl.when(j >= N_PLAIN_TILES)
    def _():
        o_ref[...] = _sigmoid(acc).astype(BF16)


def fused_proj(h, w_big, tm=1024):
    t = h.shape[0]
    return pl.pallas_call(
        _proj_kernel,
        grid=(t // tm, PROJ_WIDTH // PROJ_TILE),
        in_specs=[
            pl.BlockSpec((tm, D_MODEL), lambda i, j: (i, 0)),
            pl.BlockSpec((D_MODEL, PROJ_TILE), lambda i, j: (0, j)),
        ],
        out_specs=pl.BlockSpec((tm, PROJ_TILE), lambda i, j: (i, j)),
        out_shape=jax.ShapeDtypeStruct((t, PROJ_WIDTH), BF16),
        compiler_params=_params("parallel", "arbitrary"),
        name="fused_proj",
    )(h, w_big)


def _fox_prep_kernel(z_ref, b_ref, cq_ref, ck_ref, carry_ref):
    @pl.when(pl.program_id(1) == 0)
    def _():
        carry_ref[...] = jnp.zeros_like(carry_ref)

    ts = z_ref.shape[0]
    lf = _log_sigmoid(z_ref[...] + b_ref[...])
    r = lax.broadcasted_iota(jnp.int32, (ts, ts), 0)
    c = lax.broadcasted_iota(jnp.int32, (ts, ts), 1)
    tri = jnp.where(r >= c, 1.0, 0.0).astype(BF16)
    p1, p2, p3 = _split3(lf)
    cs = _dot(tri, p1) + _dot(tri, p2) + _dot(tri, p3) + carry_ref[...]
    carry_ref[...] = cs[ts - 1:ts, :]

    lane = lax.broadcasted_iota(jnp.int32, (ts, LANES), 1)
    cq_parts, ck_parts = [], []
    for h in range(N_HEADS):
        ch = jnp.broadcast_to(cs[:, h:h + 1], (ts, LANES))
        c1, c2, c3 = (p.astype(F32) for p in _split3(ch))
        cq = jnp.where(lane == 0, c1, jnp.where(lane == 1, c2, jnp.where(lane == 2, c3,
             jnp.where(lane < 6, 1.0, 0.0))))
        ck = jnp.where(lane < 3, 1.0, jnp.where(lane == 3, -c1, jnp.where(lane == 4, -c2,
             jnp.where(lane == 5, -c3, 0.0))))
        cq_parts.append(cq.astype(BF16))
        ck_parts.append(ck.astype(BF16))
    cq_ref[...] = jnp.concatenate(cq_parts, axis=1)
    ck_ref[...] = jnp.concatenate(ck_parts, axis=1)


def fox_prep(small3, bias_row, ts=512):
    b, s, _ = small3.shape
    ts = min(ts, s)
    return pl.pallas_call(
        _fox_prep_kernel,
        grid=(b, s // ts),
        in_specs=[
            pl.BlockSpec((None, ts, LANES), lambda bi, i: (bi, i, 0)),
            pl.BlockSpec((1, LANES), lambda bi, i: (0, 0)),
        ],
        out_specs=[
            pl.BlockSpec((None, ts, MIX_WIDTH), lambda bi, i: (bi, i, 0)),
            pl.BlockSpec((None, ts, MIX_WIDTH), lambda bi, i: (bi, i, 0)),
        ],
        out_shape=[
            jax.ShapeDtypeStruct((b, s, MIX_WIDTH), BF16),
            jax.ShapeDtypeStruct((b, s, MIX_WIDTH), BF16),
        ],
        scratch_shapes=[pltpu.VMEM((1, LANES), F32)],
        compiler_params=_params("parallel", "arbitrary"),
        name="fox_prep",
    )(small3, bias_row)


def _fox_kernel(q_ref, cq_ref, k_ref, ck_ref, v_ref, o_ref, *, tq):
    i = pl.program_id(2)
    q = jnp.concatenate([q_ref[...], cq_ref[...]], axis=1)

    def step(j, carry, masked):
        m, l, acc = carry
        off = pl.multiple_of(j * tq, tq)
        k = jnp.concatenate([k_ref[pl.ds(off, tq), :], ck_ref[pl.ds(off, tq), :]], axis=1)
        s = _dot_nt(q, k)
        if masked:
            r = lax.broadcasted_iota(jnp.int32, (tq, tq), 0)
            c = lax.broadcasted_iota(jnp.int32, (tq, tq), 1)
            s = jnp.where(r >= c, s, MASK_VALUE)
        m_new = jnp.maximum(m, jnp.max(s, axis=-1, keepdims=True))
        p = jnp.exp(s - m_new)
        alpha = jnp.exp(m - m_new)
        l = alpha * l + jnp.sum(p, axis=-1, keepdims=True)
        acc = alpha * acc + _dot(p.astype(BF16), v_ref[pl.ds(off, tq), :])
        return m_new, l, acc

    init = (jnp.full((tq, 1), MASK_VALUE, F32), jnp.zeros((tq, 1), F32),
            jnp.zeros((tq, HEAD_DV), F32))
    carry = lax.fori_loop(0, i, lambda j, cr: step(j, cr, False), init)
    _, l, acc = step(i, carry, True)
    o_ref[...] = (acc / l).astype(BF16)


def fox_attention(proj3, cq, ck, tq=512):
    b, s, _ = proj3.shape
    tq = min(tq, s)
    kernel = functools.partial(_fox_kernel, tq=tq)
    return pl.pallas_call(
        kernel,
        grid=(b, N_HEADS, s // tq),
        in_specs=[
            pl.BlockSpec((None, tq, HEAD_DV), lambda bi, h, i: (bi, i, h)),
            pl.BlockSpec((None, tq, HEAD_DV), lambda bi, h, i: (bi, i, h)),
            pl.BlockSpec((None, s, HEAD_DV), lambda bi, h, i: (bi, 0, N_HEADS + h)),
            pl.BlockSpec((None, s, HEAD_DV), lambda bi, h, i: (bi, 0, h)),
            pl.BlockSpec((None, s, HEAD_DV), lambda bi, h, i: (bi, 0, 2 * N_HEADS + h)),
        ],
        out_specs=pl.BlockSpec((None, tq, HEAD_DV), lambda bi, h, i: (bi, i, h)),
        out_shape=jax.ShapeDtypeStruct((b, s, MIX_WIDTH), BF16),
        compiler_params=_params("parallel", "parallel", "arbitrary"),
        name="fox_attention",
    )(proj3, cq, proj3, ck, proj3)


def _gla_kernel(q_ref, k_ref, v_ref, r_ref, sm_ref, wah_ref, wal_ref, ba_ref, o_ref, st_ref, *, blk):
    @pl.when(pl.program_id(1) == 0)
    def _():
        st_ref[...] = jnp.zeros_like(st_ref)

    n_chunks = blk // GLA_CHUNK
    ga = sm_ref[...]
    ga_hi = ga.astype(BF16)
    ga_lo = (ga - ga_hi.astype(F32)).astype(BF16)
    z = (_dot(ga_hi, wah_ref[...]) + _dot(ga_lo, wah_ref[...]) + _dot(ga_hi, wal_ref[...])
         + ba_ref[...])
    la = _log_sigmoid(z) * (1.0 / GLA_TAU)

    r = lax.broadcasted_iota(jnp.int32, (blk, blk), 0)
    c = lax.broadcasted_iota(jnp.int32, (blk, blk), 1)
    shift = GLA_CHUNK.bit_length() - 1
    same = jnp.right_shift(r, shift) == jnp.right_shift(c, shift)
    causal = (lax.bitcast_convert_type(r - c, jnp.uint32)
              <= lax.bitcast_convert_type(jnp.bitwise_and(r, GLA_CHUNK - 1), jnp.uint32))
    tri = jnp.where(causal, 1.0, 0.0).astype(BF16)
    ones = jnp.where(same, 1.0, 0.0).astype(BF16)
    p1, p2, p3 = _split3(la)
    bcum = _dot(tri, p1) + _dot(tri, p2) + _dot(tri, p3)
    btot = _dot(ones, p1) + _dot(ones, p2) + _dot(ones, p3)

    q = q_ref[...].astype(F32)
    k = k_ref[...].astype(F32)
    qd = (q * jnp.exp(bcum)).astype(BF16)
    ki = (k * jnp.exp(-bcum)).astype(BF16)
    ke = (k * jnp.exp(btot - bcum)).astype(BF16)
    a_end = jnp.exp(btot)
    v = v_ref[...]

    lane_head = lax.broadcasted_iota(jnp.int32, (blk, GLA_KDIM), 1) // GLA_DK
    o_intra = []
    for h in range(N_HEADS):
        qm = jnp.where(lane_head == h, qd, jnp.zeros_like(qd))
        att = _dot_nt(qm, ki)
        att = jnp.where(causal, att, 0.0).astype(BF16)
        o_intra.append(_dot(att, v[:, h * HEAD_DV:(h + 1) * HEAD_DV]))
    o_intra = jnp.concatenate(o_intra, axis=1)

    sr = lax.broadcasted_iota(jnp.int32, (MIX_WIDTH, GLA_KDIM), 0) // HEAD_DV
    sc = lax.broadcasted_iota(jnp.int32, (MIX_WIDTH, GLA_KDIM), 1) // GLA_DK
    diag = sr == sc
    st = st_ref[...]
    o_inter = []
    for ci in range(n_chunks):
        lo = ci * GLA_CHUNK
        o_inter.append(_dot_nt(qd[lo:lo + GLA_CHUNK], st.astype(BF16)))
        u = _dot_tn(v[lo:lo + GLA_CHUNK], ke[lo:lo + GLA_CHUNK])
        st = st * a_end[lo:lo + 1, :] + jnp.where(diag, u, 0.0)
    st_ref[...] = st
    o = o_intra + jnp.concatenate(o_inter, axis=0)

    gate = r_ref[...].astype(F32)
    gate = gate * _sigmoid(gate)
    outs = []
    for h in range(N_HEADS):
        oh = o[:, h * HEAD_DV:(h + 1) * HEAD_DV]
        oh = oh * lax.rsqrt(jnp.mean(oh * oh, axis=-1, keepdims=True) + NORM_EPS)
        outs.append(oh)
    o_ref[...] = (jnp.concatenate(outs, axis=1) * gate).astype(BF16)


def gla_mixer(proj3, small3, wa_hi, wa_lo, ba_row, blk=512):
    b, s, _ = proj3.shape
    blk = min(blk, s)
    kernel = functools.partial(_gla_kernel, blk=blk)
    return pl.pallas_call(
        kernel,
        grid=(b, s // blk),
        in_specs=[
            pl.BlockSpec((None, blk, GLA_KDIM), lambda bi, i: (bi, i, 6)),
            pl.BlockSpec((None, blk, GLA_KDIM), lambda bi, i: (bi, i, 7)),
            pl.BlockSpec((None, blk, MIX_WIDTH), lambda bi, i: (bi, i, 4)),
            pl.BlockSpec((None, blk, MIX_WIDTH), lambda bi, i: (bi, i, 5)),
            pl.BlockSpec((None, blk, LANES), lambda bi, i: (bi, i, 0)),
            pl.BlockSpec((LANES, GLA_KDIM), lambda bi, i: (0, 0)),
            pl.BlockSpec((LANES, GLA_KDIM), lambda bi, i: (0, 0)),
            pl.BlockSpec((1, GLA_KDIM), lambda bi, i: (0, 0)),
        ],
        out_specs=pl.BlockSpec((None, blk, MIX_WIDTH), lambda bi, i: (bi, i, 0)),
        out_shape=jax.ShapeDtypeStruct((b, s, MIX_WIDTH), BF16),
        scratch_shapes=[pltpu.VMEM((MIX_WIDTH, GLA_KDIM), F32)],
        compiler_params=_params("parallel", "arbitrary"),
        name="gla_mixer",
    )(proj3, proj3, proj3, proj3, small3, wa_hi, wa_lo, ba_row)


def _ret_kernel(q_ref, k_ref, v_ref, g_ref, cos_ref, sin_ref, o_ref, st_ref, *, blk):
    @pl.when(pl.program_id(1) == 0)
    def _():
        st_ref[...] = jnp.zeros_like(st_ref)

    row = lax.broadcasted_iota(jnp.int32, (blk, HEAD_DV), 0).astype(F32)
    r = lax.broadcasted_iota(jnp.int32, (blk, blk), 0)
    c = lax.broadcasted_iota(jnp.int32, (blk, blk), 1)
    causal = r >= c
    cosf = cos_ref[...]
    sinf = sin_ref[...]
    outs = []
    for h in range(N_HEADS):
        lg = math.log1p(-(2.0 ** (-5.0 - h)))
        sl = slice(h * HEAD_DV, (h + 1) * HEAD_DV)
        q = q_ref[:, sl].astype(F32)
        k = k_ref[:, sl].astype(F32)
        q = q * cosf + pltpu.roll(q, HEAD_DV // 2, 1) * sinf
        k = k * cosf + pltpu.roll(k, HEAD_DV // 2, 1) * sinf
        qd = (q * jnp.exp((row + 1.0) * lg)).astype(BF16)
        ki = (k * jnp.exp((row + 1.0) * (-lg))).astype(BF16)
        ke = (k * jnp.exp((blk - 1.0 - row) * lg)).astype(BF16)
        v = v_ref[:, sl]
        att = jnp.where(causal, _dot_nt(qd, ki), 0.0).astype(BF16)
        st = st_ref[h]
        o = _dot(att, v) + _dot(qd, st.astype(BF16))
        st_ref[h] = st * math.exp(blk * lg) + _dot_tn(ke, v)
        mu = jnp.mean(o, axis=-1, keepdims=True)
        d = o - mu
        var = jnp.mean(d * d, axis=-1, keepdims=True)
        gate = g_ref[:, sl].astype(F32)
        outs.append(d * lax.rsqrt(var + NORM_EPS) * (gate * _sigmoid(gate)))
    o_ref[...] = jnp.concatenate(outs, axis=1).astype(BF16)


def ret_mixer(proj3, cos_tab, sin_tab, blk=512):
    b, s, _ = proj3.shape
    blk = min(blk, s)
    kernel = functools.partial(_ret_kernel, blk=blk)
    return pl.pallas_call(
        kernel,
        grid=(b, s // blk),
        in_specs=[
            pl.BlockSpec((None, blk, MIX_WIDTH), lambda bi, i: (bi, i, 6)),
            pl.BlockSpec((None, blk, MIX_WIDTH), lambda bi, i: (bi, i, 7)),
            pl.BlockSpec((None, blk, MIX_WIDTH), lambda bi, i: (bi, i, 8)),
            pl.BlockSpec((None, blk, MIX_WIDTH), lambda bi, i: (bi, i, 9)),
            pl.BlockSpec((blk, HEAD_DV), lambda bi, i: (i, 0)),
            pl.BlockSpec((blk, HEAD_DV), lambda bi, i: (i, 0)),
        ],
        out_specs=pl.BlockSpec((None, blk, MIX_WIDTH), lambda bi, i: (bi, i, 0)),
        out_shape=jax.ShapeDtypeStruct((b, s, MIX_WIDTH), BF16),
        scratch_shapes=[pltpu.VMEM((N_HEADS, HEAD_DV, HEAD_DV), F32)],
        compiler_params=_params("parallel", "arbitrary"),
        name="ret_mixer",
    )(proj3, proj3, proj3, proj3, cos_tab, sin_tab)


def _mix_out_kernel(of_ref, og_ref, or_ref, g0_ref, g1_ref, g2_ref, x_ref, wb_ref, wo_ref, n_ref,
                    xo_ref, h_ref):
    m = (g0_ref[...].astype(F32) * _dot(of_ref[...], wb_ref[0])
         + g1_ref[...].astype(F32) * _dot(og_ref[...], wb_ref[1])
         + g2_ref[...].astype(F32) * _dot(or_ref[...], wb_ref[2]))
    xn = x_ref[...] + _dot(m.astype(BF16), wo_ref[...])
    xo_ref[...] = xn
    h_ref[...] = _rmsnorm_rows(xn, n_ref[...]).astype(BF16)


def mix_out(o_fox, o_gla, o_ret, proj, x2d, wb, wo, n2g, tm=512):
    t = x2d.shape[0]
    gate_blk = PROJ_TILE
    row = lambda i: (i, 0)
    return pl.pallas_call(
        _mix_out_kernel,
        grid=(t // tm,),
        in_specs=[
            pl.BlockSpec((tm, MIX_WIDTH), row),
            pl.BlockSpec((tm, MIX_WIDTH), row),
            pl.BlockSpec((tm, MIX_WIDTH), row),
            pl.BlockSpec((tm, gate_blk), lambda i: (i, N_PLAIN_TILES)),
            pl.BlockSpec((tm, gate_blk), lambda i: (i, N_PLAIN_TILES + 1)),
            pl.BlockSpec((tm, gate_blk), lambda i: (i, N_PLAIN_TILES + 2)),
            pl.BlockSpec((tm, D_MODEL), row),
            pl.BlockSpec((3, MIX_WIDTH, D_MODEL), lambda i: (0, 0, 0)),
            pl.BlockSpec((D_MODEL, D_MODEL), lambda i: (0, 0)),
            pl.BlockSpec((1, D_MODEL), lambda i: (0, 0)),
        ],
        out_specs=[pl.BlockSpec((tm, D_MODEL), row), pl.BlockSpec((tm, D_MODEL), row)],
        out_shape=[jax.ShapeDtypeStruct((t, D_MODEL), F32), jax.ShapeDtypeStruct((t, D_MODEL), BF16)],
        compiler_params=_params("parallel"),
        name="mix_out",
    )(o_fox, o_gla, o_ret, proj, proj, proj, x2d, wb, wo, n2g)


def _ffn_kernel(h_ref, w1_ref, w3_ref, w2_ref, x_ref, o_ref, acc_ref):
    j = pl.program_id(1)

    @pl.when(j == 0)
    def _():
        acc_ref[...] = jnp.zeros_like(acc_ref)

    h = h_ref[...]
    a = _dot(h, w1_ref[...])
    g = (a * _sigmoid(a) * _dot(h, w3_ref[...])).astype(BF16)
    acc_ref[...] += _dot(g, w2_ref[...])

    @pl.when(j == pl.num_programs(1) - 1)
    def _():
        o_ref[...] = x_ref[...] + acc_ref[...]


def dense_ffn(h, w1, w3, w2, x2d, tm=1024, tf=256):
    t = h.shape[0]
    d_ff = w1.shape[1]
    return pl.pallas_call(
        _ffn_kernel,
        grid=(t // tm, d_ff // tf),
        in_specs=[
            pl.BlockSpec((tm, D_MODEL), lambda i, j: (i, 0)),
            pl.BlockSpec((D_MODEL, tf), lambda i, j: (0, j)),
            pl.BlockSpec((D_MODEL, tf), lambda i, j: (0, j)),
            pl.BlockSpec((tf, D_MODEL), lambda i, j: (j, 0)),
            pl.BlockSpec((tm, D_MODEL), lambda i, j: (i, 0)),
        ],
        out_specs=pl.BlockSpec((tm, D_MODEL), lambda i, j: (i, 0)),
        out_shape=jax.ShapeDtypeStruct((t, D_MODEL), F32),
        scratch_shapes=[pltpu.VMEM((tm, D_MODEL), F32)],
        compiler_params=_params("parallel", "arbitrary"),
        name="dense_ffn",
    )(h, w1, w3, w2, x2d)


def _router_kernel(x_ref, g_ref, wh_ref, wl_ref, mi_ref, mf_ref, cnt_ref, run_ref):
    @pl.when(pl.program_id(0) == 0)
    def _():
        run_ref[...] = jnp.zeros_like(run_ref)

    h = _rmsnorm_rows(x_ref[...], g_ref[...])
    h_hi = h.astype(BF16)
    h_lo = (h - h_hi.astype(F32)).astype(BF16)
    logits = _dot(h_hi, wh_ref[...]) + _dot(h_lo, wh_ref[...]) + _dot(h_hi, wl_ref[...])
    tm = logits.shape[0]
    lane = lax.broadcasted_iota(jnp.int32, (tm, LANES), 1)
    lg = jnp.where(lane < N_EXPERTS, logits, -jnp.inf)
    v1 = jnp.max(lg, axis=-1, keepdims=True)
    i1 = jnp.min(jnp.where(lg == v1, lane, LANES), axis=-1, keepdims=True)
    lg2 = jnp.where(lane == i1, -jnp.inf, lg)
    v2 = jnp.max(lg2, axis=-1, keepdims=True)
    i2 = jnp.min(jnp.where(lg2 == v2, lane, LANES), axis=-1, keepdims=True)
    e = jnp.exp(v2 - v1)
    w1 = 1.0 / (1.0 + e)
    w2 = e * w1

    oh1 = lane == i1
    oh2 = lane == i2
    cnt = jnp.where(oh1, 1.0, 0.0) + jnp.where(oh2, 1.0, 0.0)
    r = lax.broadcasted_iota(jnp.int32, (tm, tm), 0)
    c = lax.broadcasted_iota(jnp.int32, (tm, tm), 1)
    earlier = jnp.where(r > c, 1.0, 0.0).astype(BF16)
    excl = _dot(earlier, cnt.astype(BF16)) + run_ref[...]
    r1 = jnp.sum(jnp.where(oh1, excl, 0.0), axis=-1, keepdims=True)
    r2 = jnp.sum(jnp.where(oh2, excl, 0.0), axis=-1, keepdims=True)
    run_ref[...] += jnp.sum(cnt, axis=0, keepdims=True)
    cnt_ref[...] = run_ref[...]

    mi_ref[...] = jnp.where(lane == 0, i1, jnp.where(lane == 1, i2, jnp.where(
        lane == 2, r1.astype(jnp.int32), jnp.where(lane == 3, r2.astype(jnp.int32), 0))))
    mf_ref[...] = jnp.where(lane == 0, w1, jnp.where(lane == 1, w2, 0.0))


def moe_router(x2d, g, wr_hi, wr_lo, tm=512):
    t = x2d.shape[0]
    return pl.pallas_call(
        _router_kernel,
        grid=(t // tm,),
        in_specs=[
            pl.BlockSpec((tm, D_MODEL), lambda i: (i, 0)),
            pl.BlockSpec((1, D_MODEL), lambda i: (0, 0)),
            pl.BlockSpec((D_MODEL, LANES), lambda i: (0, 0)),
            pl.BlockSpec((D_MODEL, LANES), lambda i: (0, 0)),
        ],
        out_specs=[
            pl.BlockSpec((tm, LANES), lambda i: (i, 0)),
            pl.BlockSpec((tm, LANES), lambda i: (i, 0)),
            pl.BlockSpec((1, LANES), lambda i: (0, 0)),
        ],
        out_shape=[
            jax.ShapeDtypeStruct((t, LANES), jnp.int32),
            jax.ShapeDtypeStruct((t, LANES), F32),
            jax.ShapeDtypeStruct((1, LANES), F32),
        ],
        scratch_shapes=[pltpu.VMEM((1, LANES), F32)],
        compiler_params=_params("arbitrary"),
        name="moe_router",
    )(x2d, g, wr_hi, wr_lo)


def _row_copy(src_ref, src_row, dst_ref, dst_row, sem):
    return pltpu.make_async_copy(src_ref.at[pl.ds(src_row, 1), :], dst_ref.at[pl.ds(dst_row, 1), :], sem)


def _dispatch_kernel(pos_ref, x_ref, xs_init_hbm, xs_hbm, sem, *, td):
    del xs_init_hbm

    def issue(t, carry):
        _row_copy(x_ref, t, xs_hbm, pos_ref[0, t], sem).start()
        _row_copy(x_ref, t, xs_hbm, pos_ref[0, td + t], sem).start()
        return carry

    lax.fori_loop(0, td, issue, 0, unroll=8)

    def drain(t, carry):
        _row_copy(x_ref, 0, xs_hbm, 0, sem).wait()
        _row_copy(x_ref, 0, xs_hbm, 0, sem).wait()
        return carry

    lax.fori_loop(0, td, drain, 0, unroll=8)


def moe_dispatch(pos_tiles, x2d, n_rows, td):
    t = x2d.shape[0]
    xs_init = jnp.zeros((n_rows, D_MODEL), F32)
    kernel = functools.partial(_dispatch_kernel, td=td)
    return pl.pallas_call(
        kernel,
        grid=(t // td,),
        in_specs=[
            pl.BlockSpec((None, 1, 2 * td), lambda i: (i, 0, 0), memory_space=pltpu.SMEM),
            pl.BlockSpec((td, D_MODEL), lambda i: (i, 0)),
            pl.BlockSpec(memory_space=pl.ANY),
        ],
        out_specs=pl.BlockSpec(memory_space=pl.ANY),
        out_shape=jax.ShapeDtypeStruct((n_rows, D_MODEL), F32),
        scratch_shapes=[pltpu.SemaphoreType.DMA(())],
        input_output_aliases={2: 0},
        compiler_params=_params("arbitrary"),
        name="moe_dispatch",
    )(pos_tiles, x2d, xs_init)


def _moe_grouped_kernel(te_ref, nu_ref, xs_ref, g_ref, w1_ref, w3_ref, w2_ref, ys_ref, h_ref, acc_ref):
    del te_ref
    j = pl.program_id(1)
    used = pl.program_id(0) < nu_ref[0]

    @pl.when(jnp.logical_and(jnp.logical_not(used), j == pl.num_programs(1) - 1))
    def _():
        ys_ref[...] = jnp.zeros_like(ys_ref)

    @pl.when(used)
    def _():
        @pl.when(j == 0)
        def _():
            h_ref[...] = _rmsnorm_rows(xs_ref[...], g_ref[...]).astype(BF16)
            acc_ref[...] = jnp.zeros_like(acc_ref)

        h = h_ref[...]
        a = _dot(h, w1_ref[...])
        g = (a * _sigmoid(a) * _dot(h, w3_ref[...])).astype(BF16)
        acc_ref[...] += _dot(g, w2_ref[...])

        @pl.when(j == pl.num_programs(1) - 1)
        def _():
            ys_ref[...] = acc_ref[...]


def moe_grouped(tile_expert, n_used, xs, g, w1, w3, w2, tm, tf=512):
    n_rows = xs.shape[0]
    nf = w1.shape[2] // tf

    def row_map(i, j, te, nu):
        return (jnp.minimum(i, nu[0] - 1), 0)

    def chunk(i, j, nu):
        return jnp.where(i < nu[0], j, nf - 1)

    return pl.pallas_call(
        _moe_grouped_kernel,
        grid_spec=pltpu.PrefetchScalarGridSpec(
            num_scalar_prefetch=2,
            grid=(n_rows // tm, nf),
            in_specs=[
                pl.BlockSpec((tm, D_MODEL), row_map),
                pl.BlockSpec((1, D_MODEL), lambda i, j, te, nu: (0, 0)),
                pl.BlockSpec((None, D_MODEL, tf), lambda i, j, te, nu: (te[i], 0, chunk(i, j, nu))),
                pl.BlockSpec((None, D_MODEL, tf), lambda i, j, te, nu: (te[i], 0, chunk(i, j, nu))),
                pl.BlockSpec((None, tf, D_MODEL), lambda i, j, te, nu: (te[i], chunk(i, j, nu), 0)),
            ],
            out_specs=pl.BlockSpec((tm, D_MODEL), lambda i, j, te, nu: (i, 0)),
            scratch_shapes=[pltpu.VMEM((tm, D_MODEL), BF16), pltpu.VMEM((tm, D_MODEL), F32)],
        ),
        out_shape=jax.ShapeDtypeStruct((n_rows, D_MODEL), F32),
        compiler_params=_params("arbitrary", "arbitrary"),
        name="moe_grouped",
    )(tile_expert, n_used, xs, g, w1, w3, w2)


def _combine_kernel(pos_ref, ys_hbm, x_ref, mf_ref, fg_ref, o_ref, buf_ref, sem, *, tc, final):
    def issue(t, carry):
        _row_copy(ys_hbm, pos_ref[0, t], buf_ref.at[0], t, sem).start()
        _row_copy(ys_hbm, pos_ref[0, tc + t], buf_ref.at[1], t, sem).start()
        return carry

    lax.fori_loop(0, tc, issue, 0, unroll=8)

    def drain(t, carry):
        _row_copy(ys_hbm, 0, buf_ref.at[0], 0, sem).wait()
        _row_copy(ys_hbm, 0, buf_ref.at[1], 0, sem).wait()
        return carry

    lax.fori_loop(0, tc, drain, 0, unroll=8)

    mf = mf_ref[...]
    y = x_ref[...] + mf[:, 0:1] * buf_ref[0] + mf[:, 1:2] * buf_ref[1]
    o_ref[...] = _rmsnorm_rows(y, fg_ref[...]) if final else y


def moe_combine(pos_tiles, ys, x2d, mf, final_g, final, tc):
    t = x2d.shape[0]
    kernel = functools.partial(_combine_kernel, tc=tc, final=final)
    return pl.pallas_call(
        kernel,
        grid=(t // tc,),
        in_specs=[
            pl.BlockSpec((None, 1, 2 * tc), lambda i: (i, 0, 0), memory_space=pltpu.SMEM),
            pl.BlockSpec(memory_space=pl.ANY),
            pl.BlockSpec((tc, D_MODEL), lambda i: (i, 0)),
            pl.BlockSpec((tc, LANES), lambda i: (i, 0)),
            pl.BlockSpec((1, D_MODEL), lambda i: (0, 0)),
        ],
        out_specs=pl.BlockSpec((tc, D_MODEL), lambda i: (i, 0)),
        out_shape=jax.ShapeDtypeStruct((t, D_MODEL), F32),
        scratch_shapes=[pltpu.VMEM((2, tc, D_MODEL), F32), pltpu.SemaphoreType.DMA(())],
        compiler_params=_params("arbitrary"),
        name="moe_combine",
    )(pos_tiles, ys, x2d, mf, final_g)


MOE_ROW_TILE = 1024
MOE_DISPATCH_TILE = 1024
MOE_COMBINE_TILE = 512


def _pos_tiles(pos1, pos2, tile):
    n = pos1.shape[0] // tile
    return jnp.concatenate([pos1.reshape(n, 1, tile), pos2.reshape(n, 1, tile)], axis=2).astype(jnp.int32)


def routed_moe(x2d, h_gain, rw_hi, rw_lo, w1, w3, w2, final_g, final):
    t = x2d.shape[0]
    tm = min(MOE_ROW_TILE, t)
    td = min(MOE_DISPATCH_TILE, t)
    tc = min(MOE_COMBINE_TILE, t)
    mi, mf, counts = moe_router(x2d, h_gain, rw_hi, rw_lo)

    cnt = counts[0, :N_EXPERTS].astype(jnp.int32)
    tiles_e = (cnt + tm - 1) // tm
    cum = jnp.cumsum(tiles_e)
    row_off = (cum - tiles_e) * tm
    n_used = cum[-1:]
    n_tiles = (2 * t) // tm + N_EXPERTS
    tile_ids = jnp.minimum(jnp.arange(n_tiles, dtype=jnp.int32), n_used[0] - 1)
    tile_expert = jnp.searchsorted(cum, tile_ids, side="right").astype(jnp.int32)
    pos1 = row_off[mi[:, 0]] + mi[:, 2]
    pos2 = row_off[mi[:, 1]] + mi[:, 3]

    xs = moe_dispatch(_pos_tiles(pos1, pos2, td), x2d, n_tiles * tm, td)
    ys = moe_grouped(tile_expert, n_used.astype(jnp.int32), xs, h_gain, w1, w3, w2, tm)
    return moe_combine(_pos_tiles(pos1, pos2, tc), ys, x2d, mf, final_g, final, tc)


def _final_norm_kernel(x_ref, g_ref, o_ref):
    o_ref[...] = _rmsnorm_rows(x_ref[...], g_ref[...])


def final_norm(x2d, g, tm=512):
    t = x2d.shape[0]
    return pl.pallas_call(
        _final_norm_kernel,
        grid=(t // tm,),
        in_specs=[pl.BlockSpec((tm, D_MODEL), lambda i: (i, 0)),
                  pl.BlockSpec((1, D_MODEL), lambda i: (0, 0))],
        out_specs=pl.BlockSpec((tm, D_MODEL), lambda i: (i, 0)),
        out_shape=jax.ShapeDtypeStruct((t, D_MODEL), F32),
        compiler_params=_params("parallel"),
        name="final_norm",
    )(x2d, g)


_IN_SPLITS = (512, 512, 512, 4, 256, 256, 512, 16, 512, 512, 512, 512, 512)


def _prep_mixer_weights(w_in, w_gate, b_forget, w_gla_a2, b_gla_a):
    cuts = [0]
    for n in _IN_SPLITS:
        cuts.append(cuts[-1] + n)
    seg = [w_in[:, cuts[i]:cuts[i + 1]] for i in range(len(_IN_SPLITS))]
    fq, fk, fv, fz, gq, gk, gv, ga, gr, rq, rk, rv, rg = seg
    w_big = jnp.concatenate(
        [fq * (HEAD_DV ** -0.5), fk, fv,
         gq * (GLA_DK ** -0.5), gk, gv, gr,
         rq, rk * (HEAD_DV ** -0.5), rv, rg,
         w_gate], axis=1).astype(BF16)
    n_small = fz.shape[1] + ga.shape[1]
    w_small = jnp.concatenate(
        [fz, ga, jnp.zeros((D_MODEL, LANES - n_small), F32)], axis=1).astype(BF16)
    bias_row = jnp.zeros((1, LANES), F32).at[0, :N_HEADS].set(b_forget)
    wa = jnp.zeros((LANES, GLA_KDIM), F32).at[N_HEADS:N_HEADS + GLA_RANK, :].set(w_gla_a2)
    wa_hi = wa.astype(BF16)
    wa_lo = (wa - wa_hi.astype(F32)).astype(BF16)
    return w_big, w_small, bias_row, wa_hi, wa_lo, b_gla_a.reshape(1, GLA_KDIM)


def _rotary_tables(s_len):
    half = HEAD_DV // 2
    inv_freq = ROPE_BASE ** (-(jnp.arange(half, dtype=F32) / half))
    ang = jnp.arange(s_len, dtype=F32)[:, None] * inv_freq[None, :]
    cos, sin = jnp.cos(ang), jnp.sin(ang)
    return jnp.concatenate([cos, cos], axis=1), jnp.concatenate([-sin, sin], axis=1)


def kernel(x, norm1_g, w_in, b_forget, w_gla_a2, b_gla_a, w_gate, w_branch, w_o, norm2_g,
           ffn_w1, ffn_w3, ffn_w2, router_w, moe_w1, moe_w3, moe_w2, final_g):
    b, s, d = x.shape
    t = b * s
    depth = norm1_g.shape[0]
    cos_tab, sin_tab = _rotary_tables(s)
    x2d = x.reshape(t, d)
    out = None
    for layer in range(depth):
        w_big, w_small, bias_row, wa_hi, wa_lo, ba_row = _prep_mixer_weights(
            w_in[layer], w_gate[layer], b_forget[layer], w_gla_a2[layer], b_gla_a[layer])
        h, small = norm_small(x2d, norm1_g[layer].reshape(1, d), w_small)
        proj = fused_proj(h, w_big)
        proj3 = proj.reshape(b, s, PROJ_WIDTH)
        small3 = small.reshape(b, s, LANES)
        cq, ck = fox_prep(small3, bias_row)
        o_fox = fox_attention(proj3, cq, ck).reshape(t, MIX_WIDTH)
        o_gla = gla_mixer(proj3, small3, wa_hi, wa_lo, ba_row).reshape(t, MIX_WIDTH)
        o_ret = ret_mixer(proj3, cos_tab, sin_tab).reshape(t, MIX_WIDTH)
        x2d, h2 = mix_out(o_fox, o_gla, o_ret, proj, x2d, w_branch[layer].astype(BF16),
                          w_o[layer].astype(BF16), norm2_g[layer].reshape(1, d))
        j = layer // 2
        last = layer == depth - 1
        if layer % 2 == 0:
            x2d = dense_ffn(h2, ffn_w1[j].astype(BF16), ffn_w3[j].astype(BF16),
                            ffn_w2[j].astype(BF16), x2d)
            if last:
                out = final_norm(x2d, final_g.reshape(1, d))
        else:
            rw = jnp.zeros((d, LANES), F32).at[:, :N_EXPERTS].set(router_w[j])
            rw_hi = rw.astype(BF16)
            rw_lo = (rw - rw_hi.astype(F32)).astype(BF16)
            x2d = routed_moe(x2d, norm2_g[layer].reshape(1, d), rw_hi, rw_lo,
                             moe_w1[j].astype(BF16), moe_w3[j].astype(BF16),
                             moe_w2[j].astype(BF16), final_g.reshape(1, d), last)
            out = x2d
    return out.reshape(b, s, d)
```
